```python
import math
import jax, jax.numpy as jnp
from jax import lax
import numpy as np

D_MODEL = 2048
BATCH = 8
SEQ = 2048
DEPTH = 1
DEC_BATCH = 128
DEC_SEQ = 4
PAST_LEN = 2048
PAGE_SIZE = 128

HEAD_DIM = 128
HEADS_PER_GROUP = 8
ATTN_GROUPS = ((128, 1), (512, 4), (2048, 16))
N_GROUPS = 3
ATTN_QKV_WIDTH = N_GROUPS * HEADS_PER_GROUP * HEAD_DIM
ATTN_OUT_WIDTH = HEADS_PER_GROUP * HEAD_DIM
ATTN_SCALE = HEAD_DIM ** -0.5
BAND_BLOCK = 128
ROPE_THETA = 10000.0
D_RNN = D_MODEL
LRU_BLOCKS = 16
LRU_BLOCK_W = D_RNN // LRU_BLOCKS
LRU_C = 8.0
CONV_W = 4
D_FF = (D_MODEL * 11) // 4
PLE_DIM = 256
EPS = 1e-6
IN_COLS = 3 * ATTN_QKV_WIDTH + 2 * D_RNN + 2 * D_MODEL

kernel_name = 'hybrid_dilated_attn_rglru_decode_step'


def rmsnorm(x, g):
    xf = x.astype(jnp.float32)
    y = xf * lax.rsqrt(jnp.mean(xf * xf, axis=-1, keepdims=True) + EPS)
    return (y * g.astype(jnp.float32)).astype(x.dtype)


def swiglu(h, w_gate, w_up, w_down):
    return (jax.nn.silu(h @ w_gate) * (h @ w_up)) @ w_down


def rope(x, pos):
    half = HEAD_DIM // 2
    inv_freq = jnp.exp(-math.log(ROPE_THETA) * jnp.arange(half, dtype=jnp.float32) / half)
    ang = pos.astype(jnp.float32)[:, None] * inv_freq[None, :]
    cos, sin = jnp.cos(ang), jnp.sin(ang)
    xf = x.astype(jnp.float32)
    x1, x2 = xf[..., :half], xf[..., half:]
    return jnp.concatenate([x1 * cos - x2 * sin, x2 * cos + x1 * sin], axis=-1).astype(x.dtype)


def group_heads(z, g):
    B, T, _ = z.shape
    return z.reshape(B, T, N_GROUPS, HEADS_PER_GROUP, HEAD_DIM)[:, :, g].transpose(0, 2, 1, 3)


def dilated_attn_full(q, k, v, window, dilation):
    B, H, S, Dh = q.shape
    L = S // dilation
    span = window // dilation
    nblk = -(-L // BAND_BLOCK)
    Lp = nblk * BAND_BLOCK

    def by_residue(t):
        return t.reshape(B, H, L, dilation, Dh).transpose(0, 1, 3, 2, 4).astype(jnp.float32)

    qb = jnp.pad(by_residue(q), ((0, 0), (0, 0), (0, 0), (0, Lp - L), (0, 0)))
    qb = qb.reshape(B, H, dilation, nblk, BAND_BLOCK, Dh)
    kpad = ((0, 0), (0, 0), (0, 0), (BAND_BLOCK, Lp - L), (0, 0))
    kr = jnp.pad(by_residue(k), kpad).reshape(B, H, dilation, nblk + 1, BAND_BLOCK, Dh)
    vr = jnp.pad(by_residue(v), kpad).reshape(B, H, dilation, nblk + 1, BAND_BLOCK, Dh)
    kb = jnp.concatenate([kr[:, :, :, :-1], kr[:, :, :, 1:]], axis=4)
    vb = jnp.concatenate([vr[:, :, :, :-1], vr[:, :, :, 1:]], axis=4)
    s = jnp.einsum('bhrnqd,bhrnkd->bhrnqk', qb, kb) * ATTN_SCALE
    q_loc = jnp.arange(BAND_BLOCK)[:, None] + BAND_BLOCK
    k_loc = jnp.arange(2 * BAND_BLOCK)[None, :]
    dist = q_loc - k_loc
    key_sub = (jnp.arange(nblk) * BAND_BLOCK)[:, None, None] - BAND_BLOCK + k_loc[None]
    valid = (dist >= 0) & (dist <= span) & (key_sub >= 0)
    s = jnp.where(valid, s, -jnp.inf)
    m = jnp.max(s, axis=-1, keepdims=True)
    e = jnp.exp(s - m)
    den = jnp.sum(e, axis=-1, keepdims=True)
    o = jnp.einsum('bhrnqk,bhrnkd->bhrnqd', e, vb) / den
    lse = (m + jnp.log(den))[..., 0]
    o = o.reshape(B, H, dilation, Lp, Dh)[:, :, :, :L].transpose(0, 1, 3, 2, 4).reshape(B, H, S, Dh)
    lse = lse.reshape(B, H, dilation, Lp)[:, :, :, :L].transpose(0, 1, 3, 2).reshape(B, H, S)
    return o, lse


def dilated_attn_cached(q, k_new, v_new, k_buf, v_buf, window, dilation):
    T = q.shape[2]
    Lb = k_buf.shape[2]
    n_keys = window // dilation + 1
    kc = jnp.concatenate([k_buf.astype(k_new.dtype), k_new], axis=2)
    vc = jnp.concatenate([v_buf.astype(v_new.dtype), v_new], axis=2)
    idx = Lb + jnp.arange(T)[:, None] - dilation * jnp.arange(n_keys)[None, :]
    valid = idx >= 0
    idx = jnp.maximum(idx, 0)
    kg = kc[:, :, idx].astype(jnp.float32)
    vg = vc[:, :, idx].astype(jnp.float32)
    s = jnp.einsum('bhtd,bhtjd->bhtj', q.astype(jnp.float32), kg) * ATTN_SCALE
    s = jnp.where(valid, s, -jnp.inf)
    m = jnp.max(s, axis=-1, keepdims=True)
    e = jnp.exp(s - m)
    den = jnp.sum(e, axis=-1, keepdims=True)
    o = jnp.einsum('bhtj,bhtjd->bhtd', e, vg) / den
    lse = (m + jnp.log(den))[..., 0]
    return o, lse


def _linear_combine(left, right):
    a_l, b_l = left
    a_r, b_r = right
    return a_l * a_r, a_r * b_l + b_r


def rglru_branch(xr, conv_prev, h_prev, conv_w, conv_b, w_r, b_r, w_i, b_i, lam):
    B, T, _ = xr.shape
    xc_in = jnp.concatenate([conv_prev.astype(xr.dtype), xr], axis=1)
    xc = conv_b
    for j in range(CONV_W):
        xc = xc + xc_in[:, j:j + T] * conv_w[j]
    xb = xc.reshape(B, T, LRU_BLOCKS, LRU_BLOCK_W)
    r = jax.nn.sigmoid(jnp.einsum('btni,nij->btnj', xb, w_r).reshape(B, T, D_RNN) + b_r)
    i = jax.nn.sigmoid(jnp.einsum('btni,nij->btnj', xb, w_i).reshape(B, T, D_RNN) + b_i)
    log_a = -LRU_C * r.astype(jnp.float32) * jax.nn.softplus(-lam.astype(jnp.float32))
    a = jnp.exp(log_a)
    mult = jnp.sqrt(jnp.maximum(-jnp.expm1(2.0 * log_a), 0.0))
    b = mult * (i * xc).astype(jnp.float32)
    b = b.at[:, 0].add(a[:, 0] * h_prev.astype(jnp.float32))
    _, h = lax.associative_scan(_linear_combine, (a, b), axis=1)
    return h, xc_in[:, -(CONV_W - 1):], h[:, -1]


def decoder_layer(x, pos, pe, lw, past, is_prompt):
    B, T, _ = x.shape
    x = x + 0.5 * swiglu(rmsnorm(x, lw['ln_ffn1']), lw['w_ffn1_gate'], lw['w_ffn1_up'], lw['w_ffn1_down'])
    h = rmsnorm(x, lw['ln_mix'])
    z = h @ lw['w_in']
    sizes = [ATTN_QKV_WIDTH] * 3 + [D_RNN, D_RNN, D_MODEL]
    q, k, v, x_rec, y_rec, g_att, g_rec = jnp.split(z, np.cumsum(sizes).tolist(), axis=-1)
    outs, lses, new_kv = [], [], []
    for g, (window, dilation) in enumerate(ATTN_GROUPS):
        qg = rope(group_heads(q, g), pos)
        kg = rope(group_heads(k, g), pos)
        vg = group_heads(v, g)
        if is_prompt:
            o, lse = dilated_attn_full(qg, kg, vg, window, dilation)
            keep = min(window, T)
            new_kv.append(jnp.stack([kg[:, :, T - keep:], vg[:, :, T - keep:]], axis=1))
        else:
            buf = past[0][g]
            o, lse = dilated_attn_cached(qg, kg, vg, buf[:, 0], buf[:, 1], window, dilation)
            new_kv.append(jnp.stack([kg, vg], axis=1))
        outs.append(o)
        lses.append(lse)
    w_grp = jax.nn.softmax(jnp.stack(lses, axis=0), axis=0)
    attn = jnp.sum(w_grp[..., None] * jnp.stack(outs, axis=0), axis=0)
    attn = attn.transpose(0, 2, 1, 3).reshape(B, T, ATTN_OUT_WIDTH).astype(x.dtype)
    if is_prompt:
        conv_prev = jnp.zeros((B, CONV_W - 1, D_RNN), x.dtype)
        h_prev = jnp.zeros((B, D_RNN), jnp.float32)
    else:
        conv_prev, h_prev = past[1], past[2]
    h_seq, new_conv, h_last = rglru_branch(x_rec, conv_prev, h_prev, lw['conv_w'], lw['conv_b'],
                                           lw['w_rgate'], lw['b_rgate'], lw['w_igate'], lw['b_igate'],
                                           lw['lru_lambda'])
    rec = (h_seq * jax.nn.gelu(y_rec.astype(jnp.float32))).astype(x.dtype)
    merged = (jax.nn.sigmoid(g_att) * (attn @ lw['w_branch_attn'])
              + jax.nn.sigmoid(g_rec) * (rec @ lw['w_branch_rec']))
    x = x + merged @ lw['w_out']
    x = x + 0.5 * swiglu(rmsnorm(x, lw['ln_ffn2']), lw['w_ffn2_gate'], lw['w_ffn2_up'], lw['w_ffn2_down'])
    x = x + jax.nn.sigmoid(rmsnorm(x, lw['ln_ple']) @ lw['w_ple_gate']) * (pe @ lw['w_ple_proj'])
    return x, new_kv, new_conv, h_last


def setup_inputs(seed: int = 0) -> dict:
    key = jax.random.key(seed)
    ks = iter(jax.random.split(key, 40))
    f32 = jnp.float32

    def nrm(shape, scale):
        return jax.random.normal(next(ks), shape, f32) * scale

    def gain(shape):
        return 1.0 + nrm(shape, 0.02)

    lb = [min(w, PAST_LEN) for w, _ in ATTN_GROUPS]
    u = jax.random.uniform(next(ks), (DEPTH, D_RNN), f32, 0.9, 0.999)
    a_base = u ** (1.0 / LRU_C)
    lam = jnp.log(a_base) - jnp.log1p(-a_base)
    return {
        'x_prompt': nrm((BATCH, SEQ, D_MODEL), 1.0),
        'x_sample': nrm((DEC_BATCH, DEC_SEQ, D_MODEL), 1.0),
        'cache_kv_w128': nrm((DEPTH, DEC_BATCH, 2, HEADS_PER_GROUP, lb[0], HEAD_DIM), 1.0),
        'cache_kv_w512': nrm((DEPTH, DEC_BATCH, 2, HEADS_PER_GROUP, lb[1], HEAD_DIM), 1.0),
        'cache_kv_w2048': nrm((DEPTH, DEC_BATCH, 2, HEADS_PER_GROUP, lb[2], HEAD_DIM), 1.0),
        'state_conv': nrm((DEPTH, DEC_BATCH, CONV_W - 1, D_RNN), 1.0),
        'state_rglru': nrm((DEPTH, DEC_BATCH, D_RNN), 0.5),
        'p_prompt': nrm((DEPTH, BATCH, SEQ, PLE_DIM), 1.0),
        'p_sample': nrm((DEPTH, DEC_BATCH, DEC_SEQ, PLE_DIM), 1.0),
        'ln_ffn1': gain((DEPTH, D_MODEL)),
        'w_ffn1_gate': nrm((DEPTH, D_MODEL, D_FF), D_MODEL ** -0.5),
        'w_ffn1_up': nrm((DEPTH, D_MODEL, D_FF), D_MODEL ** -0.5),
        'w_ffn1_down': nrm((DEPTH, D_FF, D_MODEL), D_FF ** -0.5),
        'ln_mix': gain((DEPTH, D_MODEL)),
        'w_in': nrm((DEPTH, D_MODEL, IN_COLS), D_MODEL ** -0.5),
        'conv_w': nrm((DEPTH, CONV_W, D_RNN), CONV_W ** -0.5),
        'conv_b': nrm((DEPTH, D_RNN), 0.01),
        'w_rgate': nrm((DEPTH, LRU_BLOCKS, LRU_BLOCK_W, LRU_BLOCK_W), LRU_BLOCK_W ** -0.5),
        'b_rgate': nrm((DEPTH, D_RNN), 0.01),
        'w_igate': nrm((DEPTH, LRU_BLOCKS, LRU_BLOCK_W, LRU_BLOCK_W), LRU_BLOCK_W ** -0.5),
        'b_igate': nrm((DEPTH, D_RNN), 0.01),
        'lru_lambda': lam,
        'w_branch_attn': nrm((DEPTH, ATTN_OUT_WIDTH, D_MODEL), ATTN_OUT_WIDTH ** -0.5),
        'w_branch_rec': nrm((DEPTH, D_RNN, D_MODEL), D_RNN ** -0.5),
        'w_out': nrm((DEPTH, D_MODEL, D_MODEL), D_MODEL ** -0.5),
        'ln_ffn2': gain((DEPTH, D_MODEL)),
        'w_ffn2_gate': nrm((DEPTH, D_MODEL, D_FF), D_MODEL ** -0.5),
        'w_ffn2_up': nrm((DEPTH, D_MODEL, D_FF), D_MODEL ** -0.5),
        'w_ffn2_down': nrm((DEPTH, D_FF, D_MODEL), D_FF ** -0.5),
        'ln_ple': gain((DEPTH, D_MODEL)),
        'w_ple_gate': nrm((DEPTH, D_MODEL, D_MODEL), D_MODEL ** -0.5),
        'w_ple_proj': nrm((DEPTH, PLE_DIM, D_MODEL), PLE_DIM ** -0.5),
        'ln_final': gain((D_MODEL,)),
    }


def reference(x_prompt, x_sample, cache_kv_w128, cache_kv_w512, cache_kv_w2048, state_conv, state_rglru,
              p_prompt, p_sample, ln_ffn1, w_ffn1_gate, w_ffn1_up, w_ffn1_down, ln_mix, w_in, conv_w, conv_b,
              w_rgate, b_rgate, w_igate, b_igate, lru_lambda, w_branch_attn, w_branch_rec, w_out,
              ln_ffn2, w_ffn2_gate, w_ffn2_up, w_ffn2_down, ln_ple, w_ple_gate, w_ple_proj, ln_final):
    pos_prompt = jnp.arange(x_prompt.shape[1])
    pos_sample = PAST_LEN + jnp.arange(x_sample.shape[1])
    y_p, y_s = x_prompt, x_sample
    kv_p = [[], [], []]
    kv_s = [[], [], []]
    conv_p, conv_s, rg_p, rg_s = [], [], [], []
    for l in range(DEPTH):
        lw = {
            'ln_ffn1': ln_ffn1[l], 'w_ffn1_gate': w_ffn1_gate[l], 'w_ffn1_up': w_ffn1_up[l],
            'w_ffn1_down': w_ffn1_down[l], 'ln_mix': ln_mix[l], 'w_in': w_in[l],
            'conv_w': conv_w[l], 'conv_b': conv_b[l], 'w_rgate': w_rgate[l], 'b_rgate': b_rgate[l],
            'w_igate': w_igate[l], 'b_igate': b_igate[l], 'lru_lambda': lru_lambda[l],
            'w_branch_attn': w_branch_attn[l], 'w_branch_rec': w_branch_rec[l], 'w_out': w_out[l],
            'ln_ffn2': ln_ffn2[l], 'w_ffn2_gate': w_ffn2_gate[l], 'w_ffn2_up': w_ffn2_up[l],
            'w_ffn2_down': w_ffn2_down[l], 'ln_ple': ln_ple[l], 'w_ple_gate': w_ple_gate[l],
            'w_ple_proj': w_ple_proj[l],
        }
        y_p, nkv, nconv, nh = decoder_layer(y_p, pos_prompt, p_prompt[l], lw, None, True)
        for g in range(N_GROUPS):
            kv_p[g].append(nkv[g])
        conv_p.append(nconv)
        rg_p.append(nh)
        past = ([cache_kv_w128[l], cache_kv_w512[l], cache_kv_w2048[l]], state_conv[l], state_rglru[l])
        y_s, nkv, nconv, nh = decoder_layer(y_s, pos_sample, p_sample[l], lw, past, False)
        for g in range(N_GROUPS):
            kv_s[g].append(nkv[g])
        conv_s.append(nconv)
        rg_s.append(nh)
    y_p = rmsnorm(y_p, ln_final)
    y_s = rmsnorm(y_s, ln_final)
    return (y_p, y_s,
            jnp.stack(kv_p[0]), jnp.stack(kv_p[1]), jnp.stack(kv_p[2]), jnp.stack(conv_p), jnp.stack(rg_p),
            jnp.stack(kv_s[0]), jnp.stack(kv_s[1]), jnp.stack(kv_s[2]), jnp.stack(conv_s), jnp.stack(rg_s))
```

```python
import functools
import math

import jax
import jax.numpy as jnp
from jax import lax
from jax.experimental import pallas as pl
from jax.experimental.pallas import tpu as pltpu

F32 = jnp.float32
BF16 = jnp.bfloat16

HEAD_DIM = 128
HEADS_PER_GROUP = 8
ATTN_GROUPS = ((128, 1), (512, 4), (2048, 16))
N_GROUPS = 3
QKV_WIDTH = N_GROUPS * HEADS_PER_GROUP * HEAD_DIM
ATTN_SCALE = HEAD_DIM ** -0.5
BAND_BLOCK = 128
ROPE_THETA = 10000.0
LRU_BLOCK_W = 128
LRU_C = 8.0
CONV_W = 4
EPS = 1e-6

SUBLANES = 8
LANES = 128
VMEM_LIMIT = 48 * 1024 * 1024
NEG_INF = float("-inf")


def _params(*semantics):
    return pltpu.CompilerParams(dimension_semantics=semantics, vmem_limit_bytes=VMEM_LIMIT)


def _rmsnorm(x, g):
    return x * lax.rsqrt(jnp.mean(x * x, axis=-1, keepdims=True) + EPS) * g


def _dot(a, b):
    return jnp.dot(a, b, preferred_element_type=F32)


def _dot_nt(a, b):
    return lax.dot_general(a, b, (((1,), (1,)), ((), ())), preferred_element_type=F32)


def _ffn_body(x_ref, ln_ref, wg_ref, wu_ref, wd_ref, o_ref, h_ref, acc_ref):
    f = pl.program_id(1)

    @pl.when(f == 0)
    def _():
        h_ref[...] = _rmsnorm(x_ref[...], ln_ref[...]).astype(BF16)
        acc_ref[...] = jnp.zeros_like(acc_ref)

    h = h_ref[...]
    g = _dot(h, wg_ref[...])
    u = _dot(h, wu_ref[...])
    a = (g * jax.nn.sigmoid(g)) * u
    acc_ref[...] += _dot(a.astype(BF16), wd_ref[...])

    @pl.when(f == pl.num_programs(1) - 1)
    def _():
        o_ref[...] = x_ref[...] + 0.5 * acc_ref[...]


def _ffn(x, ln, wg, wu, wd, *, tm, tf):
    m, d = x.shape
    tm = min(tm, m)
    ff = wg.shape[1]
    return pl.pallas_call(
        _ffn_body,
        grid=(m // tm, ff // tf),
        in_specs=[
            pl.BlockSpec((tm, d), lambda i, f: (i, 0)),
            pl.BlockSpec((1, d), lambda i, f: (0, 0)),
            pl.BlockSpec((d, tf), lambda i, f: (0, f)),
            pl.BlockSpec((d, tf), lambda i, f: (0, f)),
            pl.BlockSpec((tf, d), lambda i, f: (f, 0)),
        ],
        out_specs=pl.BlockSpec((tm, d), lambda i, f: (i, 0)),
        out_shape=jax.ShapeDtypeStruct((m, d), F32),
        scratch_shapes=[pltpu.VMEM((tm, d), BF16), pltpu.VMEM((tm, d), F32)],
        compiler_params=_params("parallel", "arbitrary"),
        name="ffn",
    )(x, ln, wg, wu, wd)


def _inproj_body(x_ref, ln_ref, w_ref, cos_ref, sin_ref, z_ref, h_ref, *, rope_tiles):
    j = pl.program_id(1)

    @pl.when(j == 0)
    def _():
        h_ref[...] = _rmsnorm(x_ref[...], ln_ref[...]).astype(BF16)

    z = _dot(h_ref[...], w_ref[...])

    @pl.when(j < rope_tiles)
    def _():
        cos = cos_ref[...]
        sin = sin_ref[...]
        for c in range(z.shape[1] // HEAD_DIM):
            zc = z[:, c * HEAD_DIM:(c + 1) * HEAD_DIM]
            rot = pltpu.roll(zc, HEAD_DIM // 2, axis=1)
            z_ref[:, c * HEAD_DIM:(c + 1) * HEAD_DIM] = zc * cos + rot * sin

    @pl.when(j >= rope_tiles)
    def _():
        z_ref[...] = z


def _inproj(x, ln, w, cos, sin, *, tm, tn):
    m, d = x.shape
    tm = min(tm, m)
    n = w.shape[1]
    pos_tiles = cos.shape[0] // tm
    return pl.pallas_call(
        functools.partial(_inproj_body, rope_tiles=2 * QKV_WIDTH // tn),
        grid=(m // tm, n // tn),
        in_specs=[
            pl.BlockSpec((tm, d), lambda i, j: (i, 0)),
            pl.BlockSpec((1, d), lambda i, j: (0, 0)),
            pl.BlockSpec((d, tn), lambda i, j: (0, j)),
            pl.BlockSpec((tm, HEAD_DIM), lambda i, j: (i % pos_tiles, 0)),
            pl.BlockSpec((tm, HEAD_DIM), lambda i, j: (i % pos_tiles, 0)),
        ],
        out_specs=pl.BlockSpec((tm, tn), lambda i, j: (i, j)),
        out_shape=jax.ShapeDtypeStruct((m, n), F32),
        scratch_shapes=[pltpu.VMEM((tm, d), BF16)],
        compiler_params=_params("parallel", "arbitrary"),
        name="inproj",
    )(x, ln, w, cos, sin)


def _rope_tables(pos):
    half = HEAD_DIM // 2
    inv_freq = jnp.exp(-math.log(ROPE_THETA) * jnp.arange(half, dtype=F32) / half)
    ang = pos.astype(F32)[:, None] * inv_freq[None, :]
    cos, sin = jnp.cos(ang), jnp.sin(ang)
    return jnp.concatenate([cos, cos], axis=-1), jnp.concatenate([-sin, sin], axis=-1)


def _attn_prompt_body(q1, q2, q3, k1, k2, k3, v1, v2, v3,
                      attn_ref, kv128_ref, kv512_ref, kv2048_ref,
                      o_s, m_s, l_s):
    seq = q1.shape[0]
    blk = BAND_BLOCK
    for kv_ref, k_ref, v_ref in ((kv128_ref, k1, v1), (kv512_ref, k2, v2), (kv2048_ref, k3, v3)):
        keep = kv_ref.shape[1]
        kv_ref[0] = k_ref[pl.ds(seq - keep, keep), :]
        kv_ref[1] = v_ref[pl.ds(seq - keep, keep), :]

    row = lax.broadcasted_iota(jnp.int32, (blk, blk), 0)
    col = lax.broadcasted_iota(jnp.int32, (blk, blk), 1)
    own_mask = col <= row
    prev_mask = col >= row

    groups = ((q1, k1, v1), (q2, k2, v2), (q3, k3, v3))
    for g, ((q_ref, k_ref, v_ref), (_, dil)) in enumerate(zip(groups, ATTN_GROUPS)):
        nblk = seq // dil // blk
        has_prev = nblk > 1

        def rows(ref, start, dil=dil):
            if dil == 1:
                return ref[pl.ds(start, blk), :]
            return ref[pl.ds(start, blk, stride=dil), :]

        def sub(p, carry, g=g, dil=dil, nblk=nblk, has_prev=has_prev,
                q_ref=q_ref, k_ref=k_ref, v_ref=v_ref, rows=rows):
            r = p // nblk
            n = p % nblk
            qs = r + dil * (n * blk)
            q = rows(q_ref, qs).astype(BF16)
            kc = rows(k_ref, qs).astype(BF16)
            vc = rows(v_ref, qs).astype(BF16)
            s_own = jnp.where(own_mask, _dot_nt(q, kc) * ATTN_SCALE, NEG_INF)
            m = jnp.max(s_own, axis=-1, keepdims=True)
            if has_prev:
                ps = r + dil * (jnp.maximum(n - 1, 0) * blk)
                kp = rows(k_ref, ps).astype(BF16)
                vp = rows(v_ref, ps).astype(BF16)
                pm = jnp.logical_and(prev_mask, n > 0)
                s_prev = jnp.where(pm, _dot_nt(q, kp) * ATTN_SCALE, NEG_INF)
                m = jnp.maximum(m, jnp.max(s_prev, axis=-1, keepdims=True))
            e_own = jnp.exp(s_own - m)
            l = jnp.sum(e_own, axis=-1, keepdims=True)
            acc = _dot(e_own.astype(BF16), vc)
            if has_prev:
                e_prev = jnp.exp(s_prev - m)
                l = l + jnp.sum(e_prev, axis=-1, keepdims=True)
                acc = acc + _dot(e_prev.astype(BF16), vp)
            if dil == 1:
                idx = pl.ds(qs, blk)
            else:
                idx = pl.ds(qs, blk, stride=dil)
            o_s[g, idx, :] = acc
            m_s[g, idx, :] = jnp.broadcast_to(m, (blk, HEAD_DIM))
            l_s[g, idx, :] = jnp.broadcast_to(l, (blk, HEAD_DIM))
            return carry

        lax.fori_loop(0, dil * nblk, sub, 0)

    def combine(t, carry):
        idx = pl.ds(pl.multiple_of(t * blk, blk), blk)
        ms = [m_s[g, idx, :] for g in range(N_GROUPS)]
        mx = jnp.maximum(jnp.maximum(ms[0], ms[1]), ms[2])
        num = jnp.zeros((blk, HEAD_DIM), F32)
        den = jnp.zeros((blk, HEAD_DIM), F32)
        for g in range(N_GROUPS):
            w = jnp.exp(ms[g] - mx)
            num = num + w * o_s[g, idx, :]
            den = den + w * l_s[g, idx, :]
        attn_ref[idx, :] = (num / den).astype(attn_ref.dtype)
        return carry

    lax.fori_loop(0, seq // blk, combine, 0)


def _attn_prompt(z3, batch, seq):
    hd = HEAD_DIM
    nh = HEADS_PER_GROUP

    def col_spec(base):
        return pl.BlockSpec((None, seq, hd), lambda b, h, base=base: (b, 0, base + h))

    in_specs = [col_spec((t * N_GROUPS + g) * nh) for t in range(3) for g in range(N_GROUPS)]
    keeps = [min(w, seq) for w, _ in ATTN_GROUPS]
    kv_shapes = [jax.ShapeDtypeStruct((1, batch, 2, nh, keep, hd), F32) for keep in keeps]
    kv_specs = [pl.BlockSpec((None, None, 2, None, keep, hd), lambda b, h: (0, b, 0, h, 0, 0))
                for keep in keeps]
    return pl.pallas_call(
        _attn_prompt_body,
        grid=(batch, nh),
        in_specs=in_specs,
        out_specs=[pl.BlockSpec((None, seq, hd), lambda b, h: (b, 0, h))] + kv_specs,
        out_shape=[jax.ShapeDtypeStruct((batch, seq, nh * hd), BF16)] + kv_shapes,
        scratch_shapes=[pltpu.VMEM((N_GROUPS, seq, hd), F32) for _ in range(3)],
        compiler_params=_params("parallel", "parallel"),
        name="attn_prompt",
    )(*([z3] * 9))


def _attn_sample_body(q_ref, k_ref, v_ref, c1_ref, c2_ref, c3_ref, attn_ref):
    nq = q_ref.shape[2]
    hd = HEAD_DIM
    pad = SUBLANES - nq
    row = lax.broadcasted_iota(jnp.int32, (SUBLANES, hd), 0)
    col = lax.broadcasted_iota(jnp.int32, (SUBLANES, hd), 1)

    def head(h, carry):
        parts = []
        news = []
        for g, c_ref in enumerate((c1_ref, c2_ref, c3_ref)):
            q = jnp.concatenate([q_ref[g, h], jnp.zeros((pad, hd), F32)], axis=0).astype(BF16)
            qf = q.astype(F32)
            kn = k_ref[g, h].astype(BF16).astype(F32)
            vn = v_ref[g, h].astype(BF16).astype(F32)
            if g == 0:
                s = _dot_nt(q, c_ref[0, h].astype(BF16)) * ATTN_SCALE
                parts.append((jnp.where(col >= row, s, NEG_INF), c_ref[1, h].astype(BF16)))
                for c in range(nq):
                    sn = jnp.sum(qf * kn[c:c + 1, :], axis=-1, keepdims=True) * ATTN_SCALE
                    news.append((jnp.where(row[:, :1] >= c, sn, NEG_INF), vn[c:c + 1, :]))
            else:
                for r in range(nq):
                    s = _dot_nt(q, c_ref[0, h, :, r * hd:(r + 1) * hd].astype(BF16)) * ATTN_SCALE
                    parts.append((jnp.where(row == r, s, NEG_INF),
                                  c_ref[1, h, :, r * hd:(r + 1) * hd].astype(BF16)))
                knp = jnp.concatenate([kn, jnp.zeros((pad, hd), F32)], axis=0)
                sn = jnp.sum(qf * knp, axis=-1, keepdims=True) * ATTN_SCALE
                for c in range(nq):
                    news.append((jnp.where(row[:, :1] == c, sn, NEG_INF), vn[c:c + 1, :]))
        m = parts[0][0].max(axis=-1, keepdims=True)
        for s, _ in parts[1:]:
            m = jnp.maximum(m, s.max(axis=-1, keepdims=True))
        for s, _ in news:
            m = jnp.maximum(m, s)
        l = jnp.zeros((SUBLANES, 1), F32)
        acc = jnp.zeros((SUBLANES, hd), F32)
        for s, v in parts:
            e = jnp.exp(s - m)
            l = l + jnp.sum(e, axis=-1, keepdims=True)
            acc = acc + _dot(e.astype(BF16), v)
        for s, v in news:
            e = jnp.exp(s - m)
            l = l + e
            acc = acc + e.astype(BF16).astype(F32) * v
        attn_ref[h] = (acc / l)[:nq, :]
        return carry

    lax.fori_loop(0, HEADS_PER_GROUP, head, 0)


def _attn_sample(q_s, k_s, v_s, c128, c512, c2048):
    nb, _, nh, nq, hd = q_s.shape
    new_spec = pl.BlockSpec((None, N_GROUPS, nh, nq, hd), lambda b: (b, 0, 0, 0, 0))
    c128 = c128[0]
    lb2 = c512.shape[4]
    lb3 = c2048.shape[4]
    dil2 = ATTN_GROUPS[1][1]
    dil3 = ATTN_GROUPS[2][1]
    c512 = c512[0].reshape(nb, 2, nh, lb2 // dil2, dil2 * hd)
    c2048 = c2048[0].reshape(nb, 2, nh, lb3 // dil3, dil3 * hd)
    return pl.pallas_call(
        _attn_sample_body,
        grid=(nb,),
        in_specs=[
            new_spec, new_spec, new_spec,
            pl.BlockSpec((None, 2, nh, c128.shape[3], hd), lambda b: (b, 0, 0, 0, 0)),
            pl.BlockSpec((None, 2, nh, lb2 // dil2, nq * hd), lambda b: (b, 0, 0, 0, 0)),
            pl.BlockSpec((None, 2, nh, lb3 // dil3, nq * hd), lambda b: (b, 0, 0, 0, 0)),
        ],
        out_specs=pl.BlockSpec((None, nh, nq, hd), lambda b: (b, 0, 0, 0)),
        out_shape=jax.ShapeDtypeStruct((nb, nh, nq, hd), F32),
        compiler_params=_params("parallel"),
        name="attn_sample",
    )(q_s, k_s, v_s, c128, c512, c2048)


def _rglru_body(*refs, chained):
    if chained:
        (x_ref, y_ref, cw_ref, cb_ref, wri_ref, br_ref, bi_ref, lam_ref,
         rec_ref, hout_ref, xbuf, sa, sb, carry_ref) = refs
    else:
        (x_ref, y_ref, h0_ref, cw_ref, cb_ref, wri_ref, br_ref, bi_ref, lam_ref,
         rec_ref, hout_ref, xbuf) = refs
    tc, ch = x_ref.shape
    step = pl.program_id(2) if chained else None

    if chained:
        @pl.when(step == 0)
        def _():
            xbuf[0:SUBLANES, :] = jnp.zeros((SUBLANES, ch), F32)
            carry_ref[...] = jnp.zeros_like(carry_ref)
    else:
        xbuf[0:SUBLANES, :] = jnp.zeros((SUBLANES, ch), F32)

    xbuf[SUBLANES:SUBLANES + tc, :] = x_ref[...]
    xc = cb_ref[...]
    for j in range(CONV_W):
        off = SUBLANES - (CONV_W - 1) + j
        xc = xc + xbuf[off:off + tc, :] * cw_ref[j:j + 1, :]
    if chained:
        xbuf[0:SUBLANES, :] = xbuf[tc:tc + SUBLANES, :]

    pre_r, pre_i = [], []
    for n in range(ch // LRU_BLOCK_W):
        blk = xc[:, n * LRU_BLOCK_W:(n + 1) * LRU_BLOCK_W].astype(BF16)
        ri = _dot(blk, wri_ref[n])
        pre_r.append(ri[:, :LRU_BLOCK_W])
        pre_i.append(ri[:, LRU_BLOCK_W:])
    r = jax.nn.sigmoid(jnp.concatenate(pre_r, axis=-1) + br_ref[...])
    i = jax.nn.sigmoid(jnp.concatenate(pre_i, axis=-1) + bi_ref[...])
    log_a = -LRU_C * r * jax.nn.softplus(-lam_ref[...])
    a = jnp.exp(log_a)
    th = jnp.tanh(log_a)
    mult = jnp.sqrt(jnp.maximum(-2.0 * th / (1.0 - th), 0.0))
    b = mult * (i * xc)

    row8 = lax.broadcasted_iota(jnp.int32, (tc, ch), 0) % SUBLANES
    if not chained:
        a = jnp.where(row8 < CONV_W, 1.0, a)
        b = jnp.where(row8 < CONV_W, h0_ref[...], b)
    shift = 1
    while shift < SUBLANES:
        valid = row8 >= shift
        a_prev = pltpu.roll(a, shift, axis=0)
        b_prev = pltpu.roll(b, shift, axis=0)
        b = jnp.where(valid, a * b_prev + b, b)
        a = jnp.where(valid, a * a_prev, a)
        shift *= 2

    if chained:
        sa[...] = a
        sb[...] = b

        def link(k, carry):
            idx = pl.ds(pl.multiple_of(k * SUBLANES, SUBLANES), SUBLANES)
            h = sa[idx, :] * carry + sb[idx, :]
            sb[idx, :] = h
            return jnp.broadcast_to(h[SUBLANES - 1:SUBLANES, :], (SUBLANES, ch))

        carry = lax.fori_loop(0, tc // SUBLANES, link, carry_ref[...])
        carry_ref[...] = carry
        h = sb[...]
    else:
        h = b

    rec_ref[...] = (h * jax.nn.gelu(y_ref[...])).astype(rec_ref.dtype)
    if chained:
        @pl.when(step == pl.num_programs(2) - 1)
        def _():
            hout_ref[...] = h[tc - 1:tc, :]
    else:
        hout_ref[...] = h


def _rglru_params_specs(ch, idx):
    nblk = ch // LRU_BLOCK_W
    return [
        pl.BlockSpec((CONV_W, ch), lambda *g: (0, idx(*g))),
        pl.BlockSpec((1, ch), lambda *g: (0, idx(*g))),
        pl.BlockSpec((nblk, LRU_BLOCK_W, 2 * LRU_BLOCK_W), lambda *g: (idx(*g), 0, 0)),
        pl.BlockSpec((1, ch), lambda *g: (0, idx(*g))),
        pl.BlockSpec((1, ch), lambda *g: (0, idx(*g))),
        pl.BlockSpec((1, ch), lambda *g: (0, idx(*g))),
    ]


def _rglru_prompt(z3, x_col, y_col, params, *, d_rnn, tc, ch):
    batch, seq, _ = z3.shape
    nc = d_rnn // ch
    rec, hlast = pl.pallas_call(
        functools.partial(_rglru_body, chained=True),
        grid=(batch, nc, seq // tc),
        in_specs=[
            pl.BlockSpec((None, tc, ch), lambda b, c, t: (b, t, x_col // ch + c)),
            pl.BlockSpec((None, tc, ch), lambda b, c, t: (b, t, y_col // ch + c)),
        ] + _rglru_params_specs(ch, lambda b, c, t: c),
        out_specs=[
            pl.BlockSpec((None, tc, ch), lambda b, c, t: (b, t, c)),
            pl.BlockSpec((None, 1, ch), lambda b, c, t: (b, 0, c)),
        ],
        out_shape=[jax.ShapeDtypeStruct((batch, seq, d_rnn), BF16),
                   jax.ShapeDtypeStruct((batch, 1, d_rnn), F32)],
        scratch_shapes=[pltpu.VMEM((tc + 2 * SUBLANES, ch), F32),
                        pltpu.VMEM((tc, ch), F32), pltpu.VMEM((tc, ch), F32),
                        pltpu.VMEM((SUBLANES, ch), F32)],
        compiler_params=_params("parallel", "parallel", "arbitrary"),
        name="rglru_prompt",
    )(z3, z3, *params)
    return rec, hlast


def _rglru_sample(x8, y8, h8, params, *, tc, ch):
    rows, d_rnn = x8.shape
    nc = d_rnn // ch
    row_spec = pl.BlockSpec((tc, ch), lambda c, t: (t, c))
    return pl.pallas_call(
        functools.partial(_rglru_body, chained=False),
        grid=(nc, rows // tc),
        in_specs=[row_spec, row_spec, row_spec] + _rglru_params_specs(ch, lambda c, t: c),
        out_specs=[row_spec, row_spec],
        out_shape=[jax.ShapeDtypeStruct((rows, d_rnn), F32),
                   jax.ShapeDtypeStruct((rows, d_rnn), F32)],
        scratch_shapes=[pltpu.VMEM((tc + 2 * SUBLANES, ch), F32)],
        compiler_params=_params("parallel", "parallel"),
        name="rglru_sample",
    )(x8, y8, h8, *params)


def _merge_body(x_ref, attn_ref, rec_ref, ga0_ref, ga1_ref, gr0_ref, gr1_ref,
                wba_ref, wbr_ref, wout_ref, o_ref):
    pa = _dot(attn_ref[...].astype(BF16), wba_ref[...])
    pr = _dot(rec_ref[...].astype(BF16), wbr_ref[...])
    ga = jnp.concatenate([ga0_ref[...], ga1_ref[...]], axis=-1)
    gr = jnp.concatenate([gr0_ref[...], gr1_ref[...]], axis=-1)
    merged = jax.nn.sigmoid(ga) * pa + jax.nn.sigmoid(gr) * pr
    o_ref[...] = x_ref[...] + _dot(merged.astype(BF16), wout_ref[...])


def _merge(x, attn, rec, z, g_col, wba, wbr, wout, *, tm):
    m, d = x.shape
    half = d // 2
    gb = g_col // half

    def whole(w):
        return pl.BlockSpec(w.shape, lambda i: (0, 0), pipeline_mode=pl.Buffered(1))

    def gate_spec(k):
        return pl.BlockSpec((tm, half), lambda i, k=k: (i, gb + k))

    return pl.pallas_call(
        _merge_body,
        grid=(m // tm,),
        in_specs=[
            pl.BlockSpec((tm, d), lambda i: (i, 0)),
            pl.BlockSpec((tm, attn.shape[1]), lambda i: (i, 0)),
            pl.BlockSpec((tm, rec.shape[1]), lambda i: (i, 0)),
            gate_spec(0), gate_spec(1), gate_spec(2), gate_spec(3),
            whole(wba), whole(wbr), whole(wout),
        ],
        out_specs=pl.BlockSpec((tm, d), lambda i: (i, 0)),
        out_shape=jax.ShapeDtypeStruct((m, d), F32),
        compiler_params=_params("parallel"),
        name="merge",
    )(x, attn, rec, z, z, z, z, wba, wbr, wout)


def _ple_body(x_ref, pe_ref, lnp_ref, wpg_ref, wpp_ref, lnf_ref, y_ref):
    x = x_ref[...]
    hn = _rmsnorm(x, lnp_ref[...]).astype(BF16)
    gate = jax.nn.sigmoid(_dot(hn, wpg_ref[...]))
    proj = _dot(pe_ref[...].astype(BF16), wpp_ref[...])
    y_ref[...] = _rmsnorm(x + gate * proj, lnf_ref[...])


def _ple(x, pe, lnp, wpg, wpp, lnf, *, tm):
    m, d = x.shape

    def whole(w):
        return pl.BlockSpec(w.shape, lambda i: (0, 0), pipeline_mode=pl.Buffered(1))

    return pl.pallas_call(
        _ple_body,
        grid=(m // tm,),
        in_specs=[
            pl.BlockSpec((tm, d), lambda i: (i, 0)),
            pl.BlockSpec((tm, pe.shape[1]), lambda i: (i, 0)),
            pl.BlockSpec((1, d), lambda i: (0, 0)),
            whole(wpg), whole(wpp),
            pl.BlockSpec((1, d), lambda i: (0, 0)),
        ],
        out_specs=pl.BlockSpec((tm, d), lambda i: (i, 0)),
        out_shape=jax.ShapeDtypeStruct((m, d), F32),
        compiler_params=_params("parallel"),
        name="ple",
    )(x, pe, lnp, wpg, wpp, lnf)


def kernel(x_prompt, x_sample, cache_kv_w128, cache_kv_w512, cache_kv_w2048, state_conv, state_rglru, p_prompt, p_sample, ln_ffn1, w_ffn1_gate, w_ffn1_up, w_ffn1_down, ln_mix, w_in, conv_w, conv_b, w_rgate, b_rgate, w_igate, b_igate, lru_lambda, w_branch_attn, w_branch_rec, w_out, ln_ffn2, w_ffn2_gate, w_ffn2_up, w_ffn2_down, ln_ple, w_ple_gate, w_ple_proj, ln_final):
    batch, seq, d = x_prompt.shape
    nb, nq, _ = x_sample.shape
    past_len = cache_kv_w2048.shape[4]
    d_rnn = conv_w.shape[-1]
    depth = w_in.shape[0]
    assert depth == 1

    xp = x_prompt.reshape(batch * seq, d)
    xs = x_sample.reshape(nb * nq, d)
    cos_p, sin_p = _rope_tables(jnp.arange(seq))
    cos_s, sin_s = _rope_tables(past_len + (jnp.arange(nb * nq) % nq))

    l = 0
    row = lambda v: v[l].reshape(1, -1)
    wri = jnp.concatenate([w_rgate[l], w_igate[l]], axis=-1).astype(BF16)
    rg_params = (conv_w[l], row(conv_b), wri, row(b_rgate), row(b_igate), row(lru_lambda))
    w1 = (w_ffn1_gate[l].astype(BF16), w_ffn1_up[l].astype(BF16), w_ffn1_down[l].astype(BF16))
    w2 = (w_ffn2_gate[l].astype(BF16), w_ffn2_up[l].astype(BF16), w_ffn2_down[l].astype(BF16))
    win = w_in[l].astype(BF16)
    wba = w_branch_attn[l].astype(BF16)
    wbr = w_branch_rec[l].astype(BF16)
    wout = w_out[l].astype(BF16)
    wpg = w_ple_gate[l].astype(BF16)
    wpp = w_ple_proj[l].astype(BF16)
    lnf = ln_final.reshape(1, -1)

    x_col = 3 * QKV_WIDTH
    y_col = x_col + d_rnn
    g_col = y_col + d_rnn
    tm = 512

    xp = _ffn(xp, row(ln_ffn1), *w1, tm=tm, tf=512)
    zp = _inproj(xp, row(ln_mix), win, cos_p, sin_p, tm=tm, tn=512)
    zp3 = zp.reshape(batch, seq, -1)
    attn_p, kv1_p, kv2_p, kv3_p = _attn_prompt(zp3, batch, seq)
    rec_p, hlast_p = _rglru_prompt(zp3, x_col, y_col, rg_params, d_rnn=d_rnn, tc=256, ch=1024)
    conv_p = zp3[:, seq - (CONV_W - 1):, x_col:x_col + d_rnn][None]
    xp = _merge(xp, attn_p.reshape(batch * seq, -1), rec_p.reshape(batch * seq, -1), zp, g_col,
                wba, wbr, wout, tm=256)
    xp = _ffn(xp, row(ln_ffn2), *w2, tm=tm, tf=512)
    yp = _ple(xp, p_prompt[l].reshape(batch * seq, -1), row(ln_ple), wpg, wpp, lnf, tm=256)

    xs = _ffn(xs, row(ln_ffn1), *w1, tm=tm, tf=512)
    zs = _inproj(xs, row(ln_mix), win, cos_s, sin_s, tm=tm, tn=512)
    zs3 = zs.reshape(nb, nq, -1)
    qkv = zs3[:, :, :3 * QKV_WIDTH].reshape(nb, nq, 3, N_GROUPS, HEADS_PER_GROUP, HEAD_DIM)
    qkv = qkv.transpose(2, 0, 3, 4, 1, 5)
    attn_s = _attn_sample(qkv[0], qkv[1], qkv[2], cache_kv_w128, cache_kv_w512, cache_kv_w2048)
    attn_s = attn_s.transpose(0, 2, 1, 3).reshape(nb * nq, -1)
    kv_s = [jnp.stack([qkv[1][:, g], qkv[2][:, g]], axis=1)[None] for g in range(N_GROUPS)]

    x_rec = zs3[:, :, x_col:x_col + d_rnn]
    y_rec = zs3[:, :, y_col:y_col + d_rnn]
    zero1 = jnp.zeros((nb, 1, d_rnn), F32)
    x8 = jnp.concatenate([zero1, state_conv[l], x_rec], axis=1).reshape(nb * SUBLANES, d_rnn)
    y8 = jnp.concatenate([jnp.zeros((nb, CONV_W, d_rnn), F32), y_rec], axis=1).reshape(nb * SUBLANES, d_rnn)
    h8 = jnp.concatenate([jnp.zeros((nb, CONV_W - 1, d_rnn), F32), state_rglru[l][:, None],
                          jnp.zeros((nb, nq, d_rnn), F32)], axis=1).reshape(nb * SUBLANES, d_rnn)
    rec8, hs8 = _rglru_sample(x8, y8, h8, rg_params, tc=256, ch=1024)
    rec_s = rec8.reshape(nb, SUBLANES, d_rnn)[:, CONV_W:].reshape(nb * nq, d_rnn)
    hlast_s = hs8.reshape(nb, SUBLANES, d_rnn)[:, SUBLANES - 1][None]
    conv_s = jnp.concatenate([state_conv[l], x_rec], axis=1)[:, -(CONV_W - 1):][None]
    xs = _merge(xs, attn_s, rec_s, zs, g_col, wba, wbr, wout, tm=256)
    xs = _ffn(xs, row(ln_ffn2), *w2, tm=tm, tf=512)
    ys = _ple(xs, p_sample[l].reshape(nb * nq, -1), row(ln_ple), wpg, wpp, lnf, tm=256)

    return (yp.reshape(batch, seq, d), ys.reshape(nb, nq, d),
            kv1_p, kv2_p, kv3_p, conv_p, hlast_p.reshape(1, batch, d_rnn),
            kv_s[0], kv_s[1], kv_s[2], conv_s, hlast_s)
```

```python
import functools
import math

import jax
import jax.numpy as jnp
from jax import lax
from jax.experimental import pallas as pl
from jax.experimental.pallas import tpu as pltpu

F32 = jnp.float32
BF16 = jnp.bfloat16

HEAD_DIM = 128
HEADS_PER_GROUP = 8
ATTN_GROUPS = ((128, 1), (512, 4), (2048, 16))
N_GROUPS = 3
QKV_WIDTH = N_GROUPS * HEADS_PER_GROUP * HEAD_DIM
ATTN_SCALE = HEAD_DIM ** -0.5
BAND_BLOCK = 128
ROPE_THETA = 10000.0
LRU_BLOCK_W = 128
LRU_C = 8.0
CONV_W = 4
EPS = 1e-6

SUBLANES = 8
LANES = 128
VMEM_LIMIT = 48 * 1024 * 1024
NEG_INF = float("-inf")
PROMPT_BLOCKS_PER_TRIP = 4
SAMPLE_HEADS_PER_TRIP = 4


def _params(*semantics):
    return pltpu.CompilerParams(dimension_semantics=semantics, vmem_limit_bytes=VMEM_LIMIT)


def _rmsnorm(x, g):
    return x * lax.rsqrt(jnp.mean(x * x, axis=-1, keepdims=True) + EPS) * g


def _dot(a, b):
    return jnp.dot(a, b, preferred_element_type=F32)


def _dot_nt(a, b):
    return lax.dot_general(a, b, (((1,), (1,)), ((), ())), preferred_element_type=F32)


def _ffn_body(x_ref, ln_ref, wg_ref, wu_ref, wd_ref, o_ref, h_ref, acc_ref):
    f = pl.program_id(1)

    @pl.when(f == 0)
    def _():
        h_ref[...] = _rmsnorm(x_ref[...], ln_ref[...]).astype(BF16)
        acc_ref[...] = jnp.zeros_like(acc_ref)

    h = h_ref[...]
    g = _dot(h, wg_ref[...])
    u = _dot(h, wu_ref[...])
    a = (g * jax.nn.sigmoid(g)) * u
    acc_ref[...] += _dot(a.astype(BF16), wd_ref[...])

    @pl.when(f == pl.num_programs(1) - 1)
    def _():
        o_ref[...] = x_ref[...] + 0.5 * acc_ref[...]


def _ffn(x, ln, wg, wu, wd, *, tm, tf):
    m, d = x.shape
    tm = min(tm, m)
    ff = wg.shape[1]
    return pl.pallas_call(
        _ffn_body,
        grid=(m // tm, ff // tf),
        in_specs=[
            pl.BlockSpec((tm, d), lambda i, f: (i, 0)),
            pl.BlockSpec((1, d), lambda i, f: (0, 0)),
            pl.BlockSpec((d, tf), lambda i, f: (0, f)),
            pl.BlockSpec((d, tf), lambda i, f: (0, f)),
            pl.BlockSpec((tf, d), lambda i, f: (f, 0)),
        ],
        out_specs=pl.BlockSpec((tm, d), lambda i, f: (i, 0)),
        out_shape=jax.ShapeDtypeStruct((m, d), F32),
        scratch_shapes=[pltpu.VMEM((tm, d), BF16), pltpu.VMEM((tm, d), F32)],
        compiler_params=_params("parallel", "arbitrary"),
        name="ffn",
    )(x, ln, wg, wu, wd)


def _inproj_body(x_ref, ln_ref, w_ref, cos_ref, sin_ref, z_ref, h_ref, *, rope_tiles):
    j = pl.program_id(1)

    @pl.when(j == 0)
    def _():
        h_ref[...] = _rmsnorm(x_ref[...], ln_ref[...]).astype(BF16)

    z = _dot(h_ref[...], w_ref[...])

    @pl.when(j < rope_tiles)
    def _():
        cos = cos_ref[...]
        sin = sin_ref[...]
        for c in range(z.shape[1] // HEAD_DIM):
            zc = z[:, c * HEAD_DIM:(c + 1) * HEAD_DIM]
            rot = pltpu.roll(zc, HEAD_DIM // 2, axis=1)
            z_ref[:, c * HEAD_DIM:(c + 1) * HEAD_DIM] = zc * cos + rot * sin

    @pl.when(j >= rope_tiles)
    def _():
        z_ref[...] = z


def _inproj(x, ln, w, cos, sin, *, tm, tn):
    m, d = x.shape
    tm = min(tm, m)
    n = w.shape[1]
    pos_tiles = cos.shape[0] // tm
    return pl.pallas_call(
        functools.partial(_inproj_body, rope_tiles=2 * QKV_WIDTH // tn),
        grid=(m // tm, n // tn),
        in_specs=[
            pl.BlockSpec((tm, d), lambda i, j: (i, 0)),
            pl.BlockSpec((1, d), lambda i, j: (0, 0)),
            pl.BlockSpec((d, tn), lambda i, j: (0, j)),
            pl.BlockSpec((tm, HEAD_DIM), lambda i, j: (i % pos_tiles, 0)),
            pl.BlockSpec((tm, HEAD_DIM), lambda i, j: (i % pos_tiles, 0)),
        ],
        out_specs=pl.BlockSpec((tm, tn), lambda i, j: (i, j)),
        out_shape=jax.ShapeDtypeStruct((m, n), F32),
        scratch_shapes=[pltpu.VMEM((tm, d), BF16)],
        compiler_params=_params("parallel", "arbitrary"),
        name="inproj",
    )(x, ln, w, cos, sin)


def _rope_tables(pos):
    half = HEAD_DIM // 2
    inv_freq = jnp.exp(-math.log(ROPE_THETA) * jnp.arange(half, dtype=F32) / half)
    ang = pos.astype(F32)[:, None] * inv_freq[None, :]
    cos, sin = jnp.cos(ang), jnp.sin(ang)
    return jnp.concatenate([cos, cos], axis=-1), jnp.concatenate([-sin, sin], axis=-1)


def _attn_prompt_body(q1, q2, q3, k1, k2, k3, v1, v2, v3,
                      attn_ref, kv128_ref, kv512_ref, kv2048_ref,
                      o_s, m_s, l_s):
    seq = q1.shape[0]
    blk = BAND_BLOCK
    for kv_ref, k_ref, v_ref in ((kv128_ref, k1, v1), (kv512_ref, k2, v2), (kv2048_ref, k3, v3)):
        keep = kv_ref.shape[1]
        kv_ref[0] = k_ref[pl.ds(seq - keep, keep), :]
        kv_ref[1] = v_ref[pl.ds(seq - keep, keep), :]

    row = lax.broadcasted_iota(jnp.int32, (blk, blk), 0)
    col = lax.broadcasted_iota(jnp.int32, (blk, blk), 1)
    own_mask = col <= row
    prev_mask = col >= row

    groups = ((q1, k1, v1), (q2, k2, v2), (q3, k3, v3))
    for g, ((q_ref, k_ref, v_ref), (_, dil)) in enumerate(zip(groups, ATTN_GROUPS)):
        nblk = seq // dil // blk
        has_prev = nblk > 1

        def rows(ref, start, dil=dil):
            if dil == 1:
                return ref[pl.ds(start, blk), :]
            return ref[pl.ds(start, blk, stride=dil), :]

        def sub(it, carry, g=g, dil=dil, nblk=nblk, has_prev=has_prev,
                q_ref=q_ref, k_ref=k_ref, v_ref=v_ref, rows=rows):
            work = []
            for u in range(PROMPT_BLOCKS_PER_TRIP):
                p = it * PROMPT_BLOCKS_PER_TRIP + u
                w = {"r": p // nblk, "n": p % nblk}
                w["qs"] = w["r"] + dil * (w["n"] * blk)
                w["q"] = rows(q_ref, w["qs"]).astype(BF16)
                w["kc"] = rows(k_ref, w["qs"]).astype(BF16)
                if has_prev:
                    w["ps"] = w["r"] + dil * (jnp.maximum(w["n"] - 1, 0) * blk)
                    w["kp"] = rows(k_ref, w["ps"]).astype(BF16)
                work.append(w)
            for w in work:
                w["s_own"] = jnp.where(own_mask, _dot_nt(w["q"], w["kc"]) * ATTN_SCALE, NEG_INF)
                if has_prev:
                    pm = jnp.logical_and(prev_mask, w["n"] > 0)
                    w["s_prev"] = jnp.where(pm, _dot_nt(w["q"], w["kp"]) * ATTN_SCALE, NEG_INF)
            for w in work:
                m = jnp.max(w["s_own"], axis=-1, keepdims=True)
                if has_prev:
                    m = jnp.maximum(m, jnp.max(w["s_prev"], axis=-1, keepdims=True))
                w["m"] = m
                w["e_own"] = jnp.exp(w["s_own"] - m)
                l = jnp.sum(w["e_own"], axis=-1, keepdims=True)
                if has_prev:
                    w["e_prev"] = jnp.exp(w["s_prev"] - m)
                    l = l + jnp.sum(w["e_prev"], axis=-1, keepdims=True)
                w["l"] = l
            for w in work:
                acc = _dot(w["e_own"].astype(BF16), rows(v_ref, w["qs"]).astype(BF16))
                if has_prev:
                    acc = acc + _dot(w["e_prev"].astype(BF16), rows(v_ref, w["ps"]).astype(BF16))
                w["acc"] = acc
            for w in work:
                if dil == 1:
                    idx = pl.ds(w["qs"], blk)
                else:
                    idx = pl.ds(w["qs"], blk, stride=dil)
                o_s[g, idx, :] = w["acc"]
                m_s[g, idx, :] = jnp.broadcast_to(w["m"], (blk, HEAD_DIM))
                l_s[g, idx, :] = jnp.broadcast_to(w["l"], (blk, HEAD_DIM))
            return carry

        lax.fori_loop(0, dil * nblk // PROMPT_BLOCKS_PER_TRIP, sub, 0)

    def combine(t, carry):
        idx = pl.ds(pl.multiple_of(t * blk, blk), blk)
        ms = [m_s[g, idx, :] for g in range(N_GROUPS)]
        mx = jnp.maximum(jnp.maximum(ms[0], ms[1]), ms[2])
        num = jnp.zeros((blk, HEAD_DIM), F32)
        den = jnp.zeros((blk, HEAD_DIM), F32)
        for g in range(N_GROUPS):
            w = jnp.exp(ms[g] - mx)
            num = num + w * o_s[g, idx, :]
            den = den + w * l_s[g, idx, :]
        attn_ref[idx, :] = (num / den).astype(attn_ref.dtype)
        return carry

    lax.fori_loop(0, seq // blk, combine, 0)


def _attn_prompt(z3, batch, seq):
    hd = HEAD_DIM
    nh = HEADS_PER_GROUP

    def col_spec(base):
        return pl.BlockSpec((None, seq, hd), lambda b, h, base=base: (b, 0, base + h))

    in_specs = [col_spec((t * N_GROUPS + g) * nh) for t in range(3) for g in range(N_GROUPS)]
    keeps = [min(w, seq) for w, _ in ATTN_GROUPS]
    kv_shapes = [jax.ShapeDtypeStruct((1, batch, 2, nh, keep, hd), F32) for keep in keeps]
    kv_specs = [pl.BlockSpec((None, None, 2, None, keep, hd), lambda b, h: (0, b, 0, h, 0, 0))
                for keep in keeps]
    return pl.pallas_call(
        _attn_prompt_body,
        grid=(batch, nh),
        in_specs=in_specs,
        out_specs=[pl.BlockSpec((None, seq, hd), lambda b, h: (b, 0, h))] + kv_specs,
        out_shape=[jax.ShapeDtypeStruct((batch, seq, nh * hd), BF16)] + kv_shapes,
        scratch_shapes=[pltpu.VMEM((N_GROUPS, seq, hd), F32) for _ in range(3)],
        compiler_params=_params("parallel", "parallel"),
        name="attn_prompt",
    )(*([z3] * 9))


def _attn_sample_body(q_ref, k_ref, v_ref, c1_ref, c2_ref, c3_ref, attn_ref):
    nq = q_ref.shape[2]
    hd = HEAD_DIM
    pad = SUBLANES - nq
    caches = (c1_ref, c2_ref, c3_ref)

    def cache_rows(g, kv, h):
        x = caches[g][kv, h]
        return x.reshape(-1, hd).astype(BF16)

    def mask_for(g):
        n = caches[g].shape[2] * (caches[g].shape[3] if g == 2 else 1)
        row = lax.broadcasted_iota(jnp.int32, (SUBLANES, n), 0)
        col = lax.broadcasted_iota(jnp.int32, (SUBLANES, n), 1)
        if g == 0:
            return col >= row
        step = ATTN_GROUPS[1][1] if g == 1 else caches[g].shape[3]
        assert step & (step - 1) == 0
        return jnp.bitwise_and(col, step - 1) == row

    masks = [mask_for(g) for g in range(N_GROUPS)]
    qrow = lax.broadcasted_iota(jnp.int32, (SUBLANES, 1), 0)

    def scores(h):
        parts = []
        news = []
        for g in range(N_GROUPS):
            q = jnp.concatenate([q_ref[g, h], jnp.zeros((pad, hd), F32)], axis=0).astype(BF16)
            s = _dot_nt(q, cache_rows(g, 0, h)) * ATTN_SCALE
            parts.append((jnp.where(masks[g], s, NEG_INF), g))
            qf = q.astype(F32)
            kn = k_ref[g, h].astype(BF16).astype(F32)
            vn = v_ref[g, h].astype(BF16).astype(F32)
            if g == 0:
                for c in range(nq):
                    sn = jnp.sum(qf * kn[c:c + 1, :], axis=-1, keepdims=True) * ATTN_SCALE
                    news.append((jnp.where(qrow >= c, sn, NEG_INF), vn[c:c + 1, :]))
            else:
                knp = jnp.concatenate([kn, jnp.zeros((pad, hd), F32)], axis=0)
                sn = jnp.sum(qf * knp, axis=-1, keepdims=True) * ATTN_SCALE
                for c in range(nq):
                    news.append((jnp.where(qrow == c, sn, NEG_INF), vn[c:c + 1, :]))
        return parts, news

    def softmax(parts, news):
        m = parts[0][0].max(axis=-1, keepdims=True)
        for s, _ in parts[1:]:
            m = jnp.maximum(m, s.max(axis=-1, keepdims=True))
        for s, _ in news:
            m = jnp.maximum(m, s)
        es = [jnp.exp(s - m) for s, _ in parts]
        l = jnp.zeros((SUBLANES, 1), F32)
        for e in es:
            l = l + jnp.sum(e, axis=-1, keepdims=True)
        acc = jnp.zeros((SUBLANES, hd), F32)
        for s, v in news:
            e = jnp.exp(s - m)
            l = l + e
            acc = acc + e.astype(BF16).astype(F32) * v
        return es, l, acc

    def trip(it, carry):
        heads = [it * SAMPLE_HEADS_PER_TRIP + u for u in range(SAMPLE_HEADS_PER_TRIP)]
        staged = [scores(h) for h in heads]
        soft = [softmax(parts, news) for parts, news in staged]
        outs = []
        for h, (parts, _), (es, l, acc) in zip(heads, staged, soft):
            for e, (_, g) in zip(es, parts):
                acc = acc + _dot(e.astype(BF16), cache_rows(g, 1, h))
            outs.append((acc / l)[:nq, :])
        for h, out in zip(heads, outs):
            attn_ref[h] = out
        return carry

    lax.fori_loop(0, HEADS_PER_GROUP // SAMPLE_HEADS_PER_TRIP, trip, 0)


def _attn_sample(q_s, k_s, v_s, c128, c512, c2048):
    nb, _, nh, nq, hd = q_s.shape
    new_spec = pl.BlockSpec((None, N_GROUPS, nh, nq, hd), lambda b: (b, 0, 0, 0, 0))
    for cache, (window, _) in zip((c128, c512, c2048), ATTN_GROUPS):
        assert cache.shape[4] == window, "the caches must hold a full window"
    assert nq <= ATTN_GROUPS[1][1]
    dil3 = ATTN_GROUPS[2][1]
    c2048 = c2048.reshape(1, nb, 2, nh, c2048.shape[4] // dil3, dil3, hd)
    return pl.pallas_call(
        _attn_sample_body,
        grid=(nb,),
        in_specs=[
            new_spec, new_spec, new_spec,
            pl.BlockSpec((None, None, 2, nh, c128.shape[4], hd), lambda b: (0, b, 0, 0, 0, 0)),
            pl.BlockSpec((None, None, 2, nh, c512.shape[4], hd), lambda b: (0, b, 0, 0, 0, 0)),
            pl.BlockSpec((None, None, 2, nh, c2048.shape[4], SUBLANES, hd),
                         lambda b: (0, b, 0, 0, 0, 0, 0)),
        ],
        out_specs=pl.BlockSpec((None, nh, nq, hd), lambda b: (b, 0, 0, 0)),
        out_shape=jax.ShapeDtypeStruct((nb, nh, nq, hd), F32),
        compiler_params=_params("parallel"),
        name="attn_sample",
    )(q_s, k_s, v_s, c128, c512, c2048)


def _rglru_body(*refs, chained):
    if chained:
        (x_ref, y_ref, cw_ref, cb_ref, wri_ref, br_ref, bi_ref, lam_ref,
         rec_ref, hout_ref, tail_ref, sa, sb, carry_ref) = refs
    else:
        (x_ref, y_ref, h0_ref, cw_ref, cb_ref, wri_ref, br_ref, bi_ref, lam_ref,
         rec_ref, hout_ref) = refs
    tc, ch = x_ref.shape
    tiles = (tc // SUBLANES, SUBLANES, ch)
    step = pl.program_id(2) if chained else None
    row8 = lax.broadcasted_iota(jnp.int32, tiles, 1)

    cur = x_ref[...]
    if chained:
        @pl.when(step == 0)
        def _():
            tail_ref[...] = jnp.zeros_like(tail_ref)
            carry_ref[...] = jnp.zeros_like(carry_ref)

        prev = jnp.concatenate([tail_ref[...], cur[:tc - SUBLANES, :]], axis=0)
        tail_ref[...] = cur[tc - SUBLANES:, :]
    else:
        prev = cur
    cur = cur.reshape(tiles)
    prev = prev.reshape(tiles)

    xc = cb_ref[...]
    for j in range(CONV_W):
        d = CONV_W - 1 - j
        xs = cur if d == 0 else pltpu.roll(jnp.where(row8 >= SUBLANES - d, prev, cur), d, axis=1)
        xc = xc + xs * cw_ref[j:j + 1, :]

    xc2 = xc.reshape(tc, ch)
    pre_r, pre_i = [], []
    for n in range(ch // LRU_BLOCK_W):
        blk = xc2[:, n * LRU_BLOCK_W:(n + 1) * LRU_BLOCK_W].astype(BF16)
        ri = _dot(blk, wri_ref[n])
        pre_r.append(ri[:, :LRU_BLOCK_W])
        pre_i.append(ri[:, LRU_BLOCK_W:])
    r = jax.nn.sigmoid(jnp.concatenate(pre_r, axis=-1) + br_ref[...])
    i = jax.nn.sigmoid(jnp.concatenate(pre_i, axis=-1) + bi_ref[...])
    log_a = -LRU_C * r * jax.nn.softplus(-lam_ref[...])
    a = jnp.exp(log_a)
    th = jnp.tanh(log_a)
    p = -2.0 * th
    w = 1.0 - th
    mult = jnp.where(p > 0.0, p * lax.rsqrt(p * w), 0.0)
    b = mult * (i * xc2)

    a = a.reshape(tiles)
    b = b.reshape(tiles)
    if not chained:
        a = jnp.where(row8 < CONV_W, 1.0, a)
        b = jnp.where(row8 < CONV_W, h0_ref[...].reshape(tiles), b)
    shift = 1
    while shift < SUBLANES:
        valid = row8 >= shift
        a_prev = pltpu.roll(a, shift, axis=1)
        b_prev = pltpu.roll(b, shift, axis=1)
        b = jnp.where(valid, a * b_prev + b, b)
        a = jnp.where(valid, a * a_prev, a)
        shift *= 2
    a = a.reshape(tc, ch)
    b = b.reshape(tc, ch)

    if chained:
        sa[...] = a
        sb[...] = b

        def link(k, carry):
            idx = pl.ds(pl.multiple_of(k * SUBLANES, SUBLANES), SUBLANES)
            h = sa[idx, :] * carry + sb[idx, :]
            sb[idx, :] = h
            return jnp.broadcast_to(h[SUBLANES - 1:SUBLANES, :], (SUBLANES, ch))

        carry = lax.fori_loop(0, tc // SUBLANES, link, carry_ref[...])
        carry_ref[...] = carry
        h = sb[...]
    else:
        h = b

    rec_ref[...] = (h * jax.nn.gelu(y_ref[...])).astype(rec_ref.dtype)
    if chained:
        @pl.when(step == pl.num_programs(2) - 1)
        def _():
            hout_ref[...] = h[tc - 1:tc, :]
    else:
        hout_ref[...] = h


def _rglru_params_specs(ch, idx):
    nblk = ch // LRU_BLOCK_W
    return [
        pl.BlockSpec((CONV_W, ch), lambda *g: (0, idx(*g))),
        pl.BlockSpec((1, ch), lambda *g: (0, idx(*g))),
        pl.BlockSpec((nblk, LRU_BLOCK_W, 2 * LRU_BLOCK_W), lambda *g: (idx(*g), 0, 0)),
        pl.BlockSpec((1, ch), lambda *g: (0, idx(*g))),
        pl.BlockSpec((1, ch), lambda *g: (0, idx(*g))),
        pl.BlockSpec((1, ch), lambda *g: (0, idx(*g))),
    ]


def _rglru_prompt(z3, x_col, y_col, params, *, d_rnn, tc, ch):
    batch, seq, _ = z3.shape
    nc = d_rnn // ch
    rec, hlast = pl.pallas_call(
        functools.partial(_rglru_body, chained=True),
        grid=(batch, nc, seq // tc),
        in_specs=[
            pl.BlockSpec((None, tc, ch), lambda b, c, t: (b, t, x_col // ch + c)),
            pl.BlockSpec((None, tc, ch), lambda b, c, t: (b, t, y_col // ch + c)),
        ] + _rglru_params_specs(ch, lambda b, c, t: c),
        out_specs=[
            pl.BlockSpec((None, tc, ch), lambda b, c, t: (b, t, c)),
            pl.BlockSpec((None, 1, ch), lambda b, c, t: (b, 0, c)),
        ],
        out_shape=[jax.ShapeDtypeStruct((batch, seq, d_rnn), BF16),
                   jax.ShapeDtypeStruct((batch, 1, d_rnn), F32)],
        scratch_shapes=[pltpu.VMEM((SUBLANES, ch), F32),
                        pltpu.VMEM((tc, ch), F32), pltpu.VMEM((tc, ch), F32),
                        pltpu.VMEM((SUBLANES, ch), F32)],
        compiler_params=_params("parallel", "parallel", "arbitrary"),
        name="rglru_prompt",
    )(z3, z3, *params)
    return rec, hlast


def _rglru_sample(x8, y8, h8, params, *, tc, ch):
    rows, d_rnn = x8.shape
    nc = d_rnn // ch
    row_spec = pl.BlockSpec((tc, ch), lambda c, t: (t, c))
    return pl.pallas_call(
        functools.partial(_rglru_body, chained=False),
        grid=(nc, rows // tc),
        in_specs=[row_spec, row_spec, row_spec] + _rglru_params_specs(ch, lambda c, t: c),
        out_specs=[row_spec, row_spec],
        out_shape=[jax.ShapeDtypeStruct((rows, d_rnn), F32),
                   jax.ShapeDtypeStruct((rows, d_rnn), F32)],
        compiler_params=_params("parallel", "parallel"),
        name="rglru_sample",
    )(x8, y8, h8, *params)


def _merge_body(x_ref, attn_ref, rec_ref, ga0_ref, ga1_ref, gr0_ref, gr1_ref,
                wba_ref, wbr_ref, wout_ref, o_ref):
    pa = _dot(attn_ref[...].astype(BF16), wba_ref[...])
    pr = _dot(rec_ref[...].astype(BF16), wbr_ref[...])
    ga = jnp.concatenate([ga0_ref[...], ga1_ref[...]], axis=-1)
    gr = jnp.concatenate([gr0_ref[...], gr1_ref[...]], axis=-1)
    merged = jax.nn.sigmoid(ga) * pa + jax.nn.sigmoid(gr) * pr
    o_ref[...] = x_ref[...] + _dot(merged.astype(BF16), wout_ref[...])


def _merge(x, attn, rec, z, g_col, wba, wbr, wout, *, tm):
    m, d = x.shape
    half = d // 2
    gb = g_col // half

    def whole(w):
        return pl.BlockSpec(w.shape, lambda i: (0, 0), pipeline_mode=pl.Buffered(1))

    def gate_spec(k):
        return pl.BlockSpec((tm, half), lambda i, k=k: (i, gb + k))

    return pl.pallas_call(
        _merge_body,
        grid=(m // tm,),
        in_specs=[
            pl.BlockSpec((tm, d), lambda i: (i, 0)),
            pl.BlockSpec((tm, attn.shape[1]), lambda i: (i, 0)),
            pl.BlockSpec((tm, rec.shape[1]), lambda i: (i, 0)),
            gate_spec(0), gate_spec(1), gate_spec(2), gate_spec(3),
            whole(wba), whole(wbr), whole(wout),
        ],
        out_specs=pl.BlockSpec((tm, d), lambda i: (i, 0)),
        out_shape=jax.ShapeDtypeStruct((m, d), F32),
        compiler_params=_params("parallel"),
        name="merge",
    )(x, attn, rec, z, z, z, z, wba, wbr, wout)


def _ple_body(x_ref, pe_ref, lnp_ref, wpg_ref, wpp_ref, lnf_ref, y_ref):
    x = x_ref[...]
    hn = _rmsnorm(x, lnp_ref[...]).astype(BF16)
    gate = jax.nn.sigmoid(_dot(hn, wpg_ref[...]))
    proj = _dot(pe_ref[...].astype(BF16), wpp_ref[...])
    y_ref[...] = _rmsnorm(x + gate * proj, lnf_ref[...])


def _ple(x, pe, lnp, wpg, wpp, lnf, *, tm):
    m, d = x.shape

    def whole(w):
        return pl.BlockSpec(w.shape, lambda i: (0, 0), pipeline_mode=pl.Buffered(1))

    return pl.pallas_call(
        _ple_body,
        grid=(m // tm,),
        in_specs=[
            pl.BlockSpec((tm, d), lambda i: (i, 0)),
            pl.BlockSpec((tm, pe.shape[1]), lambda i: (i, 0)),
            pl.BlockSpec((1, d), lambda i: (0, 0)),
            whole(wpg), whole(wpp),
            pl.BlockSpec((1, d), lambda i: (0, 0)),
        ],
        out_specs=pl.BlockSpec((tm, d), lambda i: (i, 0)),
        out_shape=jax.ShapeDtypeStruct((m, d), F32),
        compiler_params=_params("parallel"),
        name="ple",
    )(x, pe, lnp, wpg, wpp, lnf)


def kernel(x_prompt, x_sample, cache_kv_w128, cache_kv_w512, cache_kv_w2048, state_conv, state_rglru, p_prompt, p_sample, ln_ffn1, w_ffn1_gate, w_ffn1_up, w_ffn1_down, ln_mix, w_in, conv_w, conv_b, w_rgate, b_rgate, w_igate, b_igate, lru_lambda, w_branch_attn, w_branch_rec, w_out, ln_ffn2, w_ffn2_gate, w_ffn2_up, w_ffn2_down, ln_ple, w_ple_gate, w_ple_proj, ln_final):
    batch, seq, d = x_prompt.shape
    nb, nq, _ = x_sample.shape
    past_len = cache_kv_w2048.shape[4]
    d_rnn = conv_w.shape[-1]
    depth = w_in.shape[0]
    assert depth == 1

    xp = x_prompt.reshape(batch * seq, d)
    xs = x_sample.reshape(nb * nq, d)
    cos_p, sin_p = _rope_tables(jnp.arange(seq))
    cos_s, sin_s = _rope_tables(past_len + (jnp.arange(nb * nq) % nq))

    l = 0
    row = lambda v: v[l].reshape(1, -1)
    wri = jnp.concatenate([w_rgate[l], w_igate[l]], axis=-1).astype(BF16)
    rg_params = (conv_w[l], row(conv_b), wri, row(b_rgate), row(b_igate), row(lru_lambda))
    w1 = (w_ffn1_gate[l].astype(BF16), w_ffn1_up[l].astype(BF16), w_ffn1_down[l].astype(BF16))
    w2 = (w_ffn2_gate[l].astype(BF16), w_ffn2_up[l].astype(BF16), w_ffn2_down[l].astype(BF16))
    win = w_in[l].astype(BF16)
    wba = w_branch_attn[l].astype(BF16)
    wbr = w_branch_rec[l].astype(BF16)
    wout = w_out[l].astype(BF16)
    wpg = w_ple_gate[l].astype(BF16)
    wpp = w_ple_proj[l].astype(BF16)
    lnf = ln_final.reshape(1, -1)

    x_col = 3 * QKV_WIDTH
    y_col = x_col + d_rnn
    g_col = y_col + d_rnn
    tm = 512

    xp = _ffn(xp, row(ln_ffn1), *w1, tm=tm, tf=512)
    zp = _inproj(xp, row(ln_mix), win, cos_p, sin_p, tm=1024, tn=1024)
    zp3 = zp.reshape(batch, seq, -1)
    attn_p, kv1_p, kv2_p, kv3_p = _attn_prompt(zp3, batch, seq)
    rec_p, hlast_p = _rglru_prompt(zp3, x_col, y_col, rg_params, d_rnn=d_rnn, tc=256, ch=1024)
    conv_p = zp3[:, seq - (CONV_W - 1):, x_col:x_col + d_rnn][None]
    xp = _merge(xp, attn_p.reshape(batch * seq, -1), rec_p.reshape(batch * seq, -1), zp, g_col,
                wba, wbr, wout, tm=256)
    xp = _ffn(xp, row(ln_ffn2), *w2, tm=tm, tf=512)
    yp = _ple(xp, p_prompt[l].reshape(batch * seq, -1), row(ln_ple), wpg, wpp, lnf, tm=256)

    xs = _ffn(xs, row(ln_ffn1), *w1, tm=tm, tf=512)
    zs = _inproj(xs, row(ln_mix), win, cos_s, sin_s, tm=tm, tn=512)
    zs3 = zs.reshape(nb, nq, -1)
    qkv = zs3[:, :, :3 * QKV_WIDTH].reshape(nb, nq, 3, N_GROUPS, HEADS_PER_GROUP, HEAD_DIM)
    qkv = qkv.transpose(2, 0, 3, 4, 1, 5)
    attn_s = _attn_sample(qkv[0], qkv[1], qkv[2], cache_kv_w128, cache_kv_w512, cache_kv_w2048)
    attn_s = attn_s.transpose(0, 2, 1, 3).reshape(nb * nq, -1)
    kv_s = [jnp.stack([qkv[1][:, g], qkv[2][:, g]], axis=1)[None] for g in range(N_GROUPS)]

    x_rec = zs3[:, :, x_col:x_col + d_rnn]
    y_rec = zs3[:, :, y_col:y_col + d_rnn]
    zero1 = jnp.zeros((nb, 1, d_rnn), F32)
    x8 = jnp.concatenate([zero1, state_conv[l], x_rec], axis=1).reshape(nb * SUBLANES, d_rnn)
    y8 = jnp.concatenate([jnp.zeros((nb, CONV_W, d_rnn), F32), y_rec], axis=1).reshape(nb * SUBLANES, d_rnn)
    h8 = jnp.concatenate([jnp.zeros((nb, CONV_W - 1, d_rnn), F32), state_rglru[l][:, None],
                          jnp.zeros((nb, nq, d_rnn), F32)], axis=1).reshape(nb * SUBLANES, d_rnn)
    rec8, hs8 = _rglru_sample(x8, y8, h8, rg_params, tc=256, ch=1024)
    rec_s = rec8.reshape(nb, SUBLANES, d_rnn)[:, CONV_W:].reshape(nb * nq, d_rnn)
    hlast_s = hs8.reshape(nb, SUBLANES, d_rnn)[:, SUBLANES - 1][None]
    conv_s = jnp.concatenate([state_conv[l], x_rec], axis=1)[:, -(CONV_W - 1):][None]
    xs = _merge(xs, attn_s, rec_s, zs, g_col, wba, wbr, wout, tm=256)
    xs = _ffn(xs, row(ln_ffn2), *w2, tm=tm, tf=512)
    ys = _ple(xs, p_sample[l].reshape(nb * nq, -1), row(ln_ple), wpg, wpp, lnf, tm=256)

    return (yp.reshape(batch, seq, d), ys.reshape(nb, nq, d),
            kv1_p, kv2_p, kv3_p, conv_p, hlast_p.reshape(1, batch, d_rnn),
            kv_s[0], kv_s[1], kv_s[2], conv_s, hlast_s)
```

```python
import functools
import math

import jax
import jax.numpy as jnp
from jax import lax
from jax.experimental import pallas as pl
from jax.experimental.pallas import tpu as pltpu

F32 = jnp.float32
BF16 = jnp.bfloat16

HEAD_DIM = 128
HEADS_PER_GROUP = 8
ATTN_GROUPS = ((128, 1), (512, 4), (2048, 16))
N_GROUPS = 3
QKV_WIDTH = N_GROUPS * HEADS_PER_GROUP * HEAD_DIM
ATTN_SCALE = HEAD_DIM ** -0.5
BAND_BLOCK = 128
ROPE_THETA = 10000.0
LRU_BLOCK_W = 128
LRU_C = 8.0
CONV_W = 4
EPS = 1e-6

SUBLANES = 8
LANES = 128
VMEM_LIMIT = 48 * 1024 * 1024
FFN_VMEM_LIMIT = 56 * 1024 * 1024
NEG_INF = float("-inf")
PROMPT_BLOCKS_PER_TRIP = 8
SAMPLE_HEADS_PER_TRIP = 4
ROPE_ROW_BAND = 64


def _params(*semantics):
    return pltpu.CompilerParams(dimension_semantics=semantics, vmem_limit_bytes=VMEM_LIMIT)


def _rmsnorm(x, g):
    return x * lax.rsqrt(jnp.mean(x * x, axis=-1, keepdims=True) + EPS) * g


def _dot(a, b):
    return jnp.dot(a, b, preferred_element_type=F32)


def _dot_nt(a, b):
    return lax.dot_general(a, b, (((1,), (1,)), ((), ())), preferred_element_type=F32)


def _ffn_body(x_ref, ln_ref, wg_ref, wu_ref, wd_ref, o_ref, h_ref):
    f = pl.program_id(1)

    @pl.when(f == 0)
    def _():
        h_ref[...] = _rmsnorm(x_ref[...], ln_ref[...]).astype(BF16)
        o_ref[...] = jnp.zeros_like(o_ref)

    h = h_ref[...]
    g = _dot(h, wg_ref[...].astype(BF16))
    u = _dot(h, wu_ref[...].astype(BF16))
    a = (g * jax.nn.sigmoid(g)) * u
    o_ref[...] += _dot(a.astype(BF16), wd_ref[...].astype(BF16))

    @pl.when(f == pl.num_programs(1) - 1)
    def _():
        o_ref[...] = x_ref[...] + 0.5 * o_ref[...]


def _ffn(x, ln, wg, wu, wd, *, tm, tf):
    m, d = x.shape
    tm = min(tm, m)
    ff = wg.shape[2]
    return pl.pallas_call(
        _ffn_body,
        grid=(m // tm, ff // tf),
        in_specs=[
            pl.BlockSpec((tm, d), lambda i, f: (i, 0)),
            pl.BlockSpec((1, d), lambda i, f: (0, 0)),
            pl.BlockSpec((None, d, tf), lambda i, f: (0, 0, f)),
            pl.BlockSpec((None, d, tf), lambda i, f: (0, 0, f)),
            pl.BlockSpec((None, tf, d), lambda i, f: (0, f, 0)),
        ],
        out_specs=pl.BlockSpec((tm, d), lambda i, f: (i, 0)),
        out_shape=jax.ShapeDtypeStruct((m, d), F32),
        scratch_shapes=[pltpu.VMEM((tm, d), BF16)],
        compiler_params=pltpu.CompilerParams(dimension_semantics=("parallel", "arbitrary"),
                                             vmem_limit_bytes=FFN_VMEM_LIMIT),
        name="ffn",
    )(x, ln, wg, wu, wd)


def _inproj_body(x_ref, ln_ref, w_ref, cos_ref, sin_ref, z_ref, h_ref, zb0, zb1, *,
                 rope_tiles, col_tiles):
    j = pl.program_id(1)
    bufs = (zb0, zb1)
    assert 0 < rope_tiles < col_tiles

    def finish_rope(src):
        for r0 in range(0, z_ref.shape[0], ROPE_ROW_BAND):
            rows = slice(r0, r0 + ROPE_ROW_BAND)
            cos = cos_ref[rows, :]
            sin = sin_ref[rows, :]
            for c in range(z_ref.shape[1] // HEAD_DIM):
                cols = slice(c * HEAD_DIM, (c + 1) * HEAD_DIM)
                zc = src[rows, cols]
                z_ref[rows, cols] = zc * cos + pltpu.roll(zc, HEAD_DIM // 2, axis=1) * sin

    @pl.when(j == 0)
    def _():
        h_ref[...] = _rmsnorm(x_ref[...], ln_ref[...]).astype(BF16)
        zb0[...] = _dot(h_ref[...], w_ref[...])

    for parity in (0, 1):
        cur, prev = bufs[parity], bufs[1 - parity]
        mine = (j % 2) == parity

        @pl.when(mine & (j >= 1) & (j < rope_tiles))
        def _(cur=cur, prev=prev):
            cur[...] = _dot(h_ref[...], w_ref[...])
            finish_rope(prev)

    @pl.when(j == rope_tiles)
    def _():
        finish_rope(bufs[(rope_tiles - 1) % 2])

    @pl.when(j > rope_tiles)
    def _():
        z_ref[...] = _dot(h_ref[...], w_ref[...])


def _inproj(x, ln, w, cos, sin, *, tm, tn):
    m, d = x.shape
    tm = min(tm, m)
    n = w.shape[1]
    col_tiles = n // tn
    rope_tiles = 2 * QKV_WIDTH // tn
    pos_tiles = cos.shape[0] // tm
    return pl.pallas_call(
        functools.partial(_inproj_body, rope_tiles=rope_tiles, col_tiles=col_tiles),
        grid=(m // tm, col_tiles + 1),
        in_specs=[
            pl.BlockSpec((tm, d), lambda i, j: (i, 0)),
            pl.BlockSpec((1, d), lambda i, j: (0, 0)),
            pl.BlockSpec((d, tn), lambda i, j: (0, jnp.where(j < rope_tiles, j, j - 1))),
            pl.BlockSpec((tm, HEAD_DIM), lambda i, j: (i % pos_tiles, 0)),
            pl.BlockSpec((tm, HEAD_DIM), lambda i, j: (i % pos_tiles, 0)),
        ],
        out_specs=pl.BlockSpec((tm, tn), lambda i, j: (i, jnp.maximum(j - 1, 0))),
        out_shape=jax.ShapeDtypeStruct((m, n), F32),
        scratch_shapes=[pltpu.VMEM((tm, d), BF16), pltpu.VMEM((tm, tn), F32),
                        pltpu.VMEM((tm, tn), F32)],
        compiler_params=_params("parallel", "arbitrary"),
        name="inproj",
    )(x, ln, w, cos, sin)


def _rope_tables(pos):
    half = HEAD_DIM // 2
    inv_freq = jnp.exp(-math.log(ROPE_THETA) * jnp.arange(half, dtype=F32) / half)
    ang = pos.astype(F32)[:, None] * inv_freq[None, :]
    cos, sin = jnp.cos(ang), jnp.sin(ang)
    return jnp.concatenate([cos, cos], axis=-1), jnp.concatenate([-sin, sin], axis=-1)


def _attn_prompt_body(q1, q2, q3, k1, k2, k3, v1, v2, v3,
                      attn_ref, kv128_ref, kv512_ref, kv2048_ref,
                      o_s, m_s, l_s):
    seq = q1.shape[0]
    blk = BAND_BLOCK
    for kv_ref, k_ref, v_ref in ((kv128_ref, k1, v1), (kv512_ref, k2, v2), (kv2048_ref, k3, v3)):
        keep = kv_ref.shape[1]
        kv_ref[0] = k_ref[pl.ds(seq - keep, keep), :]
        kv_ref[1] = v_ref[pl.ds(seq - keep, keep), :]

    row = lax.broadcasted_iota(jnp.int32, (blk, blk), 0)
    col = lax.broadcasted_iota(jnp.int32, (blk, blk), 1)
    own_mask = col <= row
    prev_mask = col >= row
    ones = jnp.ones((blk, HEAD_DIM), BF16)

    groups = ((q1, k1, v1), (q2, k2, v2), (q3, k3, v3))
    for g, ((q_ref, k_ref, v_ref), (_, dil)) in enumerate(zip(groups, ATTN_GROUPS)):
        nblk = seq // dil // blk
        has_prev = nblk > 1

        def rows(ref, start, dil=dil):
            if dil == 1:
                return ref[pl.ds(start, blk), :]
            return ref[pl.ds(start, blk, stride=dil), :]

        def sub(it, carry, g=g, dil=dil, nblk=nblk, has_prev=has_prev,
                q_ref=q_ref, k_ref=k_ref, v_ref=v_ref, rows=rows):
            work = []
            for u in range(PROMPT_BLOCKS_PER_TRIP):
                p = it * PROMPT_BLOCKS_PER_TRIP + u
                w = {"r": p // nblk, "n": p % nblk}
                w["qs"] = w["r"] + dil * (w["n"] * blk)
                w["q"] = rows(q_ref, w["qs"]).astype(BF16)
                w["kc"] = rows(k_ref, w["qs"]).astype(BF16)
                if has_prev:
                    w["ps"] = w["r"] + dil * (jnp.maximum(w["n"] - 1, 0) * blk)
                    w["kp"] = rows(k_ref, w["ps"]).astype(BF16)
                work.append(w)
            for w in work:
                w["s_own"] = jnp.where(own_mask, _dot_nt(w["q"], w["kc"]) * ATTN_SCALE, NEG_INF)
                if has_prev:
                    pm = jnp.logical_and(prev_mask, w["n"] > 0)
                    w["s_prev"] = jnp.where(pm, _dot_nt(w["q"], w["kp"]) * ATTN_SCALE, NEG_INF)
            for w in work:
                m = jnp.max(w["s_own"], axis=-1, keepdims=True)
                if has_prev:
                    m = jnp.maximum(m, jnp.max(w["s_prev"], axis=-1, keepdims=True))
                w["m"] = m
                w["e_own"] = jnp.exp(w["s_own"] - m).astype(BF16)
                if has_prev:
                    w["e_prev"] = jnp.exp(w["s_prev"] - m).astype(BF16)
            for w in work:
                acc = _dot(w["e_own"], rows(v_ref, w["qs"]).astype(BF16))
                l = _dot(w["e_own"], ones)
                if has_prev:
                    acc = acc + _dot(w["e_prev"], rows(v_ref, w["ps"]).astype(BF16))
                    l = l + _dot(w["e_prev"], ones)
                w["acc"] = acc
                w["l"] = l
            for w in work:
                if dil == 1:
                    idx = pl.ds(w["qs"], blk)
                else:
                    idx = pl.ds(w["qs"], blk, stride=dil)
                o_s[g, idx, :] = w["acc"]
                m_s[g, idx, :] = jnp.broadcast_to(w["m"], (blk, HEAD_DIM))
                l_s[g, idx, :] = w["l"]
            return carry

        lax.fori_loop(0, dil * nblk // PROMPT_BLOCKS_PER_TRIP, sub, 0)

    def combine(t, carry):
        idx = pl.ds(pl.multiple_of(t * blk, blk), blk)
        ms = [m_s[g, idx, :] for g in range(N_GROUPS)]
        mx = jnp.maximum(jnp.maximum(ms[0], ms[1]), ms[2])
        num = jnp.zeros((blk, HEAD_DIM), F32)
        den = jnp.zeros((blk, HEAD_DIM), F32)
        for g in range(N_GROUPS):
            w = jnp.exp(ms[g] - mx)
            num = num + w * o_s[g, idx, :]
            den = den + w * l_s[g, idx, :]
        attn_ref[idx, :] = (num / den).astype(attn_ref.dtype)
        return carry

    lax.fori_loop(0, seq // blk, combine, 0)


def _attn_prompt(z3, batch, seq):
    hd = HEAD_DIM
    nh = HEADS_PER_GROUP

    def col_spec(base):
        return pl.BlockSpec((None, seq, hd), lambda b, h, base=base: (b, 0, base + h))

    in_specs = [col_spec((t * N_GROUPS + g) * nh) for t in range(3) for g in range(N_GROUPS)]
    keeps = [min(w, seq) for w, _ in ATTN_GROUPS]
    kv_shapes = [jax.ShapeDtypeStruct((1, batch, 2, nh, keep, hd), F32) for keep in keeps]
    kv_specs = [pl.BlockSpec((None, None, 2, None, keep, hd), lambda b, h: (0, b, 0, h, 0, 0))
                for keep in keeps]
    return pl.pallas_call(
        _attn_prompt_body,
        grid=(batch, nh),
        in_specs=in_specs,
        out_specs=[pl.BlockSpec((None, seq, hd), lambda b, h: (b, 0, h))] + kv_specs,
        out_shape=[jax.ShapeDtypeStruct((batch, seq, nh * hd), BF16)] + kv_shapes,
        scratch_shapes=[pltpu.VMEM((N_GROUPS, seq, hd), F32) for _ in range(3)],
        compiler_params=_params("parallel", "parallel"),
        name="attn_prompt",
    )(*([z3] * 9))


def _attn_sample_body(q_ref, k_ref, v_ref, c1_ref, c2_ref, c3_ref, attn_ref):
    nq = q_ref.shape[2]
    hd = HEAD_DIM
    pad = SUBLANES - nq
    caches = (c1_ref, c2_ref, c3_ref)

    def cache_rows(g, kv, h):
        x = caches[g][kv, h]
        return x.reshape(-1, hd).astype(BF16)

    def mask_for(g):
        n = caches[g].shape[2] * (caches[g].shape[3] if g == 2 else 1)
        row = lax.broadcasted_iota(jnp.int32, (SUBLANES, n), 0)
        col = lax.broadcasted_iota(jnp.int32, (SUBLANES, n), 1)
        if g == 0:
            return col >= row
        step = ATTN_GROUPS[1][1] if g == 1 else caches[g].shape[3]
        assert step & (step - 1) == 0
        return jnp.bitwise_and(col, step - 1) == row

    masks = [mask_for(g) for g in range(N_GROUPS)]
    qrow = lax.broadcasted_iota(jnp.int32, (SUBLANES, 1), 0)

    def scores(h):
        parts = []
        news = []
        for g in range(N_GROUPS):
            q = jnp.concatenate([q_ref[g, h], jnp.zeros((pad, hd), F32)], axis=0).astype(BF16)
            s = _dot_nt(q, cache_rows(g, 0, h)) * ATTN_SCALE
            parts.append((jnp.where(masks[g], s, NEG_INF), g))
            qf = q.astype(F32)
            kn = k_ref[g, h].astype(BF16).astype(F32)
            vn = v_ref[g, h].astype(BF16).astype(F32)
            if g == 0:
                for c in range(nq):
                    sn = jnp.sum(qf * kn[c:c + 1, :], axis=-1, keepdims=True) * ATTN_SCALE
                    news.append((jnp.where(qrow >= c, sn, NEG_INF), vn[c:c + 1, :]))
            else:
                knp = jnp.concatenate([kn, jnp.zeros((pad, hd), F32)], axis=0)
                sn = jnp.sum(qf * knp, axis=-1, keepdims=True) * ATTN_SCALE
                for c in range(nq):
                    news.append((jnp.where(qrow == c, sn, NEG_INF), vn[c:c + 1, :]))
        return parts, news

    def softmax(parts, news):
        m = parts[0][0].max(axis=-1, keepdims=True)
        for s, _ in parts[1:]:
            m = jnp.maximum(m, s.max(axis=-1, keepdims=True))
        for s, _ in news:
            m = jnp.maximum(m, s)
        es = [jnp.exp(s - m) for s, _ in parts]
        l = jnp.zeros((SUBLANES, 1), F32)
        for e in es:
            l = l + jnp.sum(e, axis=-1, keepdims=True)
        acc = jnp.zeros((SUBLANES, hd), F32)
        for s, v in news:
            e = jnp.exp(s - m)
            l = l + e
            acc = acc + e.astype(BF16).astype(F32) * v
        return es, l, acc

    def trip(it, carry):
        heads = [it * SAMPLE_HEADS_PER_TRIP + u for u in range(SAMPLE_HEADS_PER_TRIP)]
        staged = [scores(h) for h in heads]
        soft = [softmax(parts, news) for parts, news in staged]
        outs = []
        for h, (parts, _), (es, l, acc) in zip(heads, staged, soft):
            for e, (_, g) in zip(es, parts):
                acc = acc + _dot(e.astype(BF16), cache_rows(g, 1, h))
            outs.append((acc / l)[:nq, :])
        for h, out in zip(heads, outs):
            attn_ref[h] = out
        return carry

    lax.fori_loop(0, HEADS_PER_GROUP // SAMPLE_HEADS_PER_TRIP, trip, 0)


def _attn_sample(q_s, k_s, v_s, c128, c512, c2048):
    nb, _, nh, nq, hd = q_s.shape
    new_spec = pl.BlockSpec((None, N_GROUPS, nh, nq, hd), lambda b: (b, 0, 0, 0, 0))
    for cache, (window, _) in zip((c128, c512, c2048), ATTN_GROUPS):
        assert cache.shape[4] == window, "the caches must hold a full window"
    assert nq <= ATTN_GROUPS[1][1]
    dil3 = ATTN_GROUPS[2][1]
    c2048 = c2048.reshape(1, nb, 2, nh, c2048.shape[4] // dil3, dil3, hd)
    return pl.pallas_call(
        _attn_sample_body,
        grid=(nb,),
        in_specs=[
            new_spec, new_spec, new_spec,
            pl.BlockSpec((None, None, 2, nh, c128.shape[4], hd), lambda b: (0, b, 0, 0, 0, 0)),
            pl.BlockSpec((None, None, 2, nh, c512.shape[4], hd), lambda b: (0, b, 0, 0, 0, 0)),
            pl.BlockSpec((None, None, 2, nh, c2048.shape[4], SUBLANES, hd),
                         lambda b: (0, b, 0, 0, 0, 0, 0)),
        ],
        out_specs=pl.BlockSpec((None, nh, nq, hd), lambda b: (b, 0, 0, 0)),
        out_shape=jax.ShapeDtypeStruct((nb, nh, nq, hd), F32),
        compiler_params=_params("parallel"),
        name="attn_sample",
    )(q_s, k_s, v_s, c128, c512, c2048)


def _rglru_body(*refs, chained):
    if chained:
        (x_ref, y_ref, cw_ref, cb_ref, wri_ref, br_ref, bi_ref, lam_ref,
         rec_ref, hout_ref, tail_ref, sa, sb, carry_ref) = refs
    else:
        (x_ref, y_ref, h0_ref, cw_ref, cb_ref, wri_ref, br_ref, bi_ref, lam_ref,
         rec_ref, hout_ref) = refs
    tc, ch = x_ref.shape
    tiles = (tc // SUBLANES, SUBLANES, ch)
    step = pl.program_id(2) if chained else None
    row8 = lax.broadcasted_iota(jnp.int32, tiles, 1)

    cur = x_ref[...]
    if chained:
        @pl.when(step == 0)
        def _():
            tail_ref[...] = jnp.zeros_like(tail_ref)
            carry_ref[...] = jnp.zeros_like(carry_ref)

        prev = jnp.concatenate([tail_ref[...], cur[:tc - SUBLANES, :]], axis=0)
        tail_ref[...] = cur[tc - SUBLANES:, :]
    else:
        prev = cur
    cur = cur.reshape(tiles)
    prev = prev.reshape(tiles)

    xc = cb_ref[...]
    for j in range(CONV_W):
        d = CONV_W - 1 - j
        xs = cur if d == 0 else pltpu.roll(jnp.where(row8 >= SUBLANES - d, prev, cur), d, axis=1)
        xc = xc + xs * cw_ref[j:j + 1, :]

    xc2 = xc.reshape(tc, ch)
    pre_r, pre_i = [], []
    for n in range(ch // LRU_BLOCK_W):
        blk = xc2[:, n * LRU_BLOCK_W:(n + 1) * LRU_BLOCK_W].astype(BF16)
        ri = _dot(blk, wri_ref[n])
        pre_r.append(ri[:, :LRU_BLOCK_W])
        pre_i.append(ri[:, LRU_BLOCK_W:])
    r = jax.nn.sigmoid(jnp.concatenate(pre_r, axis=-1) + br_ref[...])
    i = jax.nn.sigmoid(jnp.concatenate(pre_i, axis=-1) + bi_ref[...])
    log_a = r * (-LRU_C * jax.nn.softplus(-lam_ref[...]))
    a = jnp.exp(log_a)
    th = jnp.tanh(log_a)
    p = -2.0 * th
    w = 1.0 - th
    mult = jnp.where(p > 0.0, p * lax.rsqrt(p * w), 0.0)
    b = mult * (i * xc2)

    a = a.reshape(tiles)
    b = b.reshape(tiles)
    if not chained:
        a = jnp.where(row8 < CONV_W, 1.0, a)
        b = jnp.where(row8 < CONV_W, h0_ref[...].reshape(tiles), b)
    shift = 1
    while shift < SUBLANES:
        valid = row8 >= shift
        a_prev = pltpu.roll(a, shift, axis=1)
        b_prev = pltpu.roll(b, shift, axis=1)
        b = jnp.where(valid, a * b_prev + b, b)
        a = jnp.where(valid, a * a_prev, a)
        shift *= 2
    a = a.reshape(tc, ch)
    b = b.reshape(tc, ch)

    if chained:
        sa[...] = a
        sb[...] = b

        def link(k, carry):
            idx = pl.ds(pl.multiple_of(k * SUBLANES, SUBLANES), SUBLANES)
            h = sa[idx, :] * carry + sb[idx, :]
            sb[idx, :] = h
            return jnp.broadcast_to(h[SUBLANES - 1:SUBLANES, :], (SUBLANES, ch))

        carry = lax.fori_loop(0, tc // SUBLANES, link, carry_ref[...])
        carry_ref[...] = carry
        h = sb[...]
    else:
        h = b

    rec_ref[...] = (h * jax.nn.gelu(y_ref[...])).astype(rec_ref.dtype)
    if chained:
        @pl.when(step == pl.num_programs(2) - 1)
        def _():
            hout_ref[...] = h[tc - 1:tc, :]
    else:
        hout_ref[...] = h


def _rglru_params_specs(ch, idx):
    nblk = ch // LRU_BLOCK_W
    return [
        pl.BlockSpec((CONV_W, ch), lambda *g: (0, idx(*g))),
        pl.BlockSpec((1, ch), lambda *g: (0, idx(*g))),
        pl.BlockSpec((nblk, LRU_BLOCK_W, 2 * LRU_BLOCK_W), lambda *g: (idx(*g), 0, 0)),
        pl.BlockSpec((1, ch), lambda *g: (0, idx(*g))),
        pl.BlockSpec((1, ch), lambda *g: (0, idx(*g))),
        pl.BlockSpec((1, ch), lambda *g: (0, idx(*g))),
    ]


def _rglru_prompt(z3, x_col, y_col, params, *, d_rnn, tc, ch):
    batch, seq, _ = z3.shape
    nc = d_rnn // ch
    rec, hlast = pl.pallas_call(
        functools.partial(_rglru_body, chained=True),
        grid=(batch, nc, seq // tc),
        in_specs=[
            pl.BlockSpec((None, tc, ch), lambda b, c, t: (b, t, x_col // ch + c)),
            pl.BlockSpec((None, tc, ch), lambda b, c, t: (b, t, y_col // ch + c)),
        ] + _rglru_params_specs(ch, lambda b, c, t: c),
        out_specs=[
            pl.BlockSpec((None, tc, ch), lambda b, c, t: (b, t, c)),
            pl.BlockSpec((None, 1, ch), lambda b, c, t: (b, 0, c)),
        ],
        out_shape=[jax.ShapeDtypeStruct((batch, seq, d_rnn), BF16),
                   jax.ShapeDtypeStruct((batch, 1, d_rnn), F32)],
        scratch_shapes=[pltpu.VMEM((SUBLANES, ch), F32),
                        pltpu.VMEM((tc, ch), F32), pltpu.VMEM((tc, ch), F32),
                        pltpu.VMEM((SUBLANES, ch), F32)],
        compiler_params=_params("parallel", "parallel", "arbitrary"),
        name="rglru_prompt",
    )(z3, z3, *params)
    return rec, hlast


def _rglru_sample(x8, y8, h8, params, *, tc, ch):
    rows, d_rnn = x8.shape
    nc = d_rnn // ch
    row_spec = pl.BlockSpec((tc, ch), lambda c, t: (t, c))
    return pl.pallas_call(
        functools.partial(_rglru_body, chained=False),
        grid=(nc, rows // tc),
        in_specs=[row_spec, row_spec, row_spec] + _rglru_params_specs(ch, lambda c, t: c),
        out_specs=[row_spec, row_spec],
        out_shape=[jax.ShapeDtypeStruct((rows, d_rnn), F32),
                   jax.ShapeDtypeStruct((rows, d_rnn), F32)],
        compiler_params=_params("parallel", "parallel"),
        name="rglru_sample",
    )(x8, y8, h8, *params)


def _merge_body(x_ref, attn_ref, rec_ref, ga0_ref, ga1_ref, gr0_ref, gr1_ref,
                wba_ref, wbr_ref, wout_ref, o_ref):
    pa = _dot(attn_ref[...].astype(BF16), wba_ref[...])
    pr = _dot(rec_ref[...].astype(BF16), wbr_ref[...])
    ga = jnp.concatenate([ga0_ref[...], ga1_ref[...]], axis=-1)
    gr = jnp.concatenate([gr0_ref[...], gr1_ref[...]], axis=-1)
    merged = jax.nn.sigmoid(ga) * pa + jax.nn.sigmoid(gr) * pr
    o_ref[...] = x_ref[...] + _dot(merged.astype(BF16), wout_ref[...])


def _merge(x, attn, rec, z, g_col, wba, wbr, wout, *, tm):
    m, d = x.shape
    half = d // 2
    gb = g_col // half

    def whole(w):
        return pl.BlockSpec(w.shape, lambda i: (0, 0), pipeline_mode=pl.Buffered(1))

    def gate_spec(k):
        return pl.BlockSpec((tm, half), lambda i, k=k: (i, gb + k))

    return pl.pallas_call(
        _merge_body,
        grid=(m // tm,),
        in_specs=[
            pl.BlockSpec((tm, d), lambda i: (i, 0)),
            pl.BlockSpec((tm, attn.shape[1]), lambda i: (i, 0)),
            pl.BlockSpec((tm, rec.shape[1]), lambda i: (i, 0)),
            gate_spec(0), gate_spec(1), gate_spec(2), gate_spec(3),
            whole(wba), whole(wbr), whole(wout),
        ],
        out_specs=pl.BlockSpec((tm, d), lambda i: (i, 0)),
        out_shape=jax.ShapeDtypeStruct((m, d), F32),
        compiler_params=_params("parallel"),
        name="merge",
    )(x, attn, rec, z, z, z, z, wba, wbr, wout)


def _ple_body(x_ref, pe_ref, lnp_ref, wpg_ref, wpp_ref, lnf_ref, y_ref):
    x = x_ref[...]
    hn = _rmsnorm(x, lnp_ref[...]).astype(BF16)
    gate = jax.nn.sigmoid(_dot(hn, wpg_ref[...]))
    proj = _dot(pe_ref[...].astype(BF16), wpp_ref[...])
    y_ref[...] = _rmsnorm(x + gate * proj, lnf_ref[...])


def _ple(x, pe, lnp, wpg, wpp, lnf, *, tm):
    m, d = x.shape

    def whole(w):
        return pl.BlockSpec(w.shape, lambda i: (0, 0), pipeline_mode=pl.Buffered(1))

    return pl.pallas_call(
        _ple_body,
        grid=(m // tm,),
        in_specs=[
            pl.BlockSpec((tm, d), lambda i: (i, 0)),
            pl.BlockSpec((tm, pe.shape[1]), lambda i: (i, 0)),
            pl.BlockSpec((1, d), lambda i: (0, 0)),
            whole(wpg), whole(wpp),
            pl.BlockSpec((1, d), lambda i: (0, 0)),
        ],
        out_specs=pl.BlockSpec((tm, d), lambda i: (i, 0)),
        out_shape=jax.ShapeDtypeStruct((m, d), F32),
        compiler_params=_params("parallel"),
        name="ple",
    )(x, pe, lnp, wpg, wpp, lnf)


def kernel(x_prompt, x_sample, cache_kv_w128, cache_kv_w512, cache_kv_w2048, state_conv, state_rglru, p_prompt, p_sample, ln_ffn1, w_ffn1_gate, w_ffn1_up, w_ffn1_down, ln_mix, w_in, conv_w, conv_b, w_rgate, b_rgate, w_igate, b_igate, lru_lambda, w_branch_attn, w_branch_rec, w_out, ln_ffn2, w_ffn2_gate, w_ffn2_up, w_ffn2_down, ln_ple, w_ple_gate, w_ple_proj, ln_final):
    batch, seq, d = x_prompt.shape
    nb, nq, _ = x_sample.shape
    past_len = cache_kv_w2048.shape[4]
    d_rnn = conv_w.shape[-1]
    depth = w_in.shape[0]
    assert depth == 1

    xp = x_prompt.reshape(batch * seq, d)
    xs = x_sample.reshape(nb * nq, d)
    cos_p, sin_p = _rope_tables(jnp.arange(seq))
    cos_s, sin_s = _rope_tables(past_len + (jnp.arange(nb * nq) % nq))

    l = 0
    row = lambda v: v[l].reshape(1, -1)
    wri = jnp.concatenate([w_rgate[l], w_igate[l]], axis=-1).astype(BF16)
    rg_params = (conv_w[l], row(conv_b), wri, row(b_rgate), row(b_igate), row(lru_lambda))
    w1 = (w_ffn1_gate, w_ffn1_up, w_ffn1_down)
    w2 = (w_ffn2_gate, w_ffn2_up, w_ffn2_down)
    ffn_tiles = dict(tm=1024, tf=256)
    win = w_in[l].astype(BF16)
    wba = w_branch_attn[l].astype(BF16)
    wbr = w_branch_rec[l].astype(BF16)
    wout = w_out[l].astype(BF16)
    wpg = w_ple_gate[l].astype(BF16)
    wpp = w_ple_proj[l].astype(BF16)
    lnf = ln_final.reshape(1, -1)

    x_col = 3 * QKV_WIDTH
    y_col = x_col + d_rnn
    g_col = y_col + d_rnn
    tm = 512

    xp = _ffn(xp, row(ln_ffn1), *w1, **ffn_tiles)
    zp = _inproj(xp, row(ln_mix), win, cos_p, sin_p, tm=1024, tn=1024)
    zp3 = zp.reshape(batch, seq, -1)
    attn_p, kv1_p, kv2_p, kv3_p = _attn_prompt(zp3, batch, seq)
    rec_p, hlast_p = _rglru_prompt(zp3, x_col, y_col, rg_params, d_rnn=d_rnn, tc=256, ch=1024)
    conv_p = zp3[:, seq - (CONV_W - 1):, x_col:x_col + d_rnn][None]
    xp = _merge(xp, attn_p.reshape(batch * seq, -1), rec_p.reshape(batch * seq, -1), zp, g_col,
                wba, wbr, wout, tm=256)
    xp = _ffn(xp, row(ln_ffn2), *w2, **ffn_tiles)
    yp = _ple(xp, p_prompt[l].reshape(batch * seq, -1), row(ln_ple), wpg, wpp, lnf, tm=256)

    xs = _ffn(xs, row(ln_ffn1), *w1, **ffn_tiles)
    zs = _inproj(xs, row(ln_mix), win, cos_s, sin_s, tm=tm, tn=512)
    zs3 = zs.reshape(nb, nq, -1)
    qkv = zs3[:, :, :3 * QKV_WIDTH].reshape(nb, nq, 3, N_GROUPS, HEADS_PER_GROUP, HEAD_DIM)
    qkv = qkv.transpose(2, 0, 3, 4, 1, 5)
    attn_s = _attn_sample(qkv[0], qkv[1], qkv[2], cache_kv_w128, cache_kv_w512, cache_kv_w2048)
    attn_s = attn_s.transpose(0, 2, 1, 3).reshape(nb * nq, -1)
    kv_s = [jnp.stack([qkv[1][:, g], qkv[2][:, g]], axis=1)[None] for g in range(N_GROUPS)]

    x_rec = zs3[:, :, x_col:x_col + d_rnn]
    y_rec = zs3[:, :, y_col:y_col + d_rnn]
    zero1 = jnp.zeros((nb, 1, d_rnn), F32)
    x8 = jnp.concatenate([zero1, state_conv[l], x_rec], axis=1).reshape(nb * SUBLANES, d_rnn)
    y8 = jnp.concatenate([jnp.zeros((nb, CONV_W, d_rnn), F32), y_rec], axis=1).reshape(nb * SUBLANES, d_rnn)
    h8 = jnp.concatenate([jnp.zeros((nb, CONV_W - 1, d_rnn), F32), state_rglru[l][:, None],
                          jnp.zeros((nb, nq, d_rnn), F32)], axis=1).reshape(nb * SUBLANES, d_rnn)
    rec8, hs8 = _rglru_sample(x8, y8, h8, rg_params, tc=256, ch=1024)
    rec_s = rec8.reshape(nb, SUBLANES, d_rnn)[:, CONV_W:].reshape(nb * nq, d_rnn)
    hlast_s = hs8.reshape(nb, SUBLANES, d_rnn)[:, SUBLANES - 1][None]
    conv_s = jnp.concatenate([state_conv[l], x_rec], axis=1)[:, -(CONV_W - 1):][None]
    xs = _merge(xs, attn_s, rec_s, zs, g_col, wba, wbr, wout, tm=256)
    xs = _ffn(xs, row(ln_ffn2), *w2, **ffn_tiles)
    ys = _ple(xs, p_sample[l].reshape(nb * nq, -1), row(ln_ple), wpg, wpp, lnf, tm=256)

    return (yp.reshape(batch, seq, d), ys.reshape(nb, nq, d),
            kv1_p, kv2_p, kv3_p, conv_p, hlast_p.reshape(1, batch, d_rnn),
            kv_s[0], kv_s[1], kv_s[2], conv_s, hlast_s)
```

```python
import functools
import math

import jax
import jax.numpy as jnp
from jax import lax
from jax.experimental import pallas as pl
from jax.experimental.pallas import tpu as pltpu

F32 = jnp.float32
BF16 = jnp.bfloat16

HEAD_DIM = 128
HEADS_PER_GROUP = 8
ATTN_GROUPS = ((128, 1), (512, 4), (2048, 16))
N_GROUPS = 3
QKV_WIDTH = N_GROUPS * HEADS_PER_GROUP * HEAD_DIM
ATTN_SCALE = HEAD_DIM ** -0.5
BAND_BLOCK = 128
ROPE_THETA = 10000.0
LRU_BLOCK_W = 128
LRU_C = 8.0
CONV_W = 4
EPS = 1e-6

SUBLANES = 8
LANES = 128
VMEM_LIMIT = 48 * 1024 * 1024
FFN_VMEM_LIMIT = 56 * 1024 * 1024
NEG_INF = float("-inf")
PROMPT_BLOCKS_PER_TRIP = 8
SAMPLE_HEADS_PER_TRIP = 4
ROPE_ROW_BAND = 64


def _params(*semantics):
    return pltpu.CompilerParams(dimension_semantics=semantics, vmem_limit_bytes=VMEM_LIMIT)


def _rmsnorm(x, g):
    return x * lax.rsqrt(jnp.mean(x * x, axis=-1, keepdims=True) + EPS) * g


def _dot(a, b):
    return jnp.dot(a, b, preferred_element_type=F32)


def _dot_nt(a, b):
    return lax.dot_general(a, b, (((1,), (1,)), ((), ())), preferred_element_type=F32)


def _ffn_body(x_ref, ln_ref, wg_ref, wu_ref, wd_ref, o_ref, h_ref):
    f = pl.program_id(1)

    @pl.when(f == 0)
    def _():
        h_ref[...] = _rmsnorm(x_ref[...], ln_ref[...]).astype(BF16)
        o_ref[...] = jnp.zeros_like(o_ref)

    h = h_ref[...]
    g = _dot(h, wg_ref[...].astype(BF16))
    u = _dot(h, wu_ref[...].astype(BF16))
    a = (g * jax.nn.sigmoid(g)) * u
    o_ref[...] += _dot(a.astype(BF16), wd_ref[...].astype(BF16))

    @pl.when(f == pl.num_programs(1) - 1)
    def _():
        o_ref[...] = x_ref[...] + 0.5 * o_ref[...]


def _ffn(x, ln, wg, wu, wd, *, tm, tf):
    m, d = x.shape
    tm = min(tm, m)
    ff = wg.shape[2]
    return pl.pallas_call(
        _ffn_body,
        grid=(m // tm, ff // tf),
        in_specs=[
            pl.BlockSpec((tm, d), lambda i, f: (i, 0)),
            pl.BlockSpec((1, d), lambda i, f: (0, 0)),
            pl.BlockSpec((None, d, tf), lambda i, f: (0, 0, f)),
            pl.BlockSpec((None, d, tf), lambda i, f: (0, 0, f)),
            pl.BlockSpec((None, tf, d), lambda i, f: (0, f, 0)),
        ],
        out_specs=pl.BlockSpec((tm, d), lambda i, f: (i, 0)),
        out_shape=jax.ShapeDtypeStruct((m, d), F32),
        scratch_shapes=[pltpu.VMEM((tm, d), BF16)],
        compiler_params=pltpu.CompilerParams(dimension_semantics=("parallel", "arbitrary"),
                                             vmem_limit_bytes=FFN_VMEM_LIMIT),
        name="ffn",
    )(x, ln, wg, wu, wd)


def _inproj_body(x_ref, ln_ref, w_ref, cos_ref, sin_ref, z_ref, h_ref, zb0, zb1, *,
                 rope_tiles, col_tiles):
    j = pl.program_id(1)
    bufs = (zb0, zb1)
    assert 0 < rope_tiles < col_tiles

    def finish_rope(src):
        for r0 in range(0, z_ref.shape[0], ROPE_ROW_BAND):
            rows = slice(r0, r0 + ROPE_ROW_BAND)
            cos = cos_ref[rows, :]
            sin = sin_ref[rows, :]
            for c in range(z_ref.shape[1] // HEAD_DIM):
                cols = slice(c * HEAD_DIM, (c + 1) * HEAD_DIM)
                zc = src[rows, cols]
                z_ref[rows, cols] = zc * cos + pltpu.roll(zc, HEAD_DIM // 2, axis=1) * sin

    @pl.when(j == 0)
    def _():
        h_ref[...] = _rmsnorm(x_ref[...], ln_ref[...]).astype(BF16)
        zb0[...] = _dot(h_ref[...], w_ref[...])

    for parity in (0, 1):
        cur, prev = bufs[parity], bufs[1 - parity]
        mine = (j % 2) == parity

        @pl.when(mine & (j >= 1) & (j < rope_tiles))
        def _(cur=cur, prev=prev):
            cur[...] = _dot(h_ref[...], w_ref[...])
            finish_rope(prev)

    @pl.when(j == rope_tiles)
    def _():
        finish_rope(bufs[(rope_tiles - 1) % 2])

    @pl.when(j > rope_tiles)
    def _():
        z_ref[...] = _dot(h_ref[...], w_ref[...])


def _inproj(x, ln, w, cos, sin, *, tm, tn):
    m, d = x.shape
    tm = min(tm, m)
    n = w.shape[1]
    col_tiles = n // tn
    rope_tiles = 2 * QKV_WIDTH // tn
    pos_tiles = cos.shape[0] // tm
    return pl.pallas_call(
        functools.partial(_inproj_body, rope_tiles=rope_tiles, col_tiles=col_tiles),
        grid=(m // tm, col_tiles + 1),
        in_specs=[
            pl.BlockSpec((tm, d), lambda i, j: (i, 0)),
            pl.BlockSpec((1, d), lambda i, j: (0, 0)),
            pl.BlockSpec((d, tn), lambda i, j: (0, jnp.where(j < rope_tiles, j, j - 1))),
            pl.BlockSpec((tm, HEAD_DIM), lambda i, j: (i % pos_tiles, 0)),
            pl.BlockSpec((tm, HEAD_DIM), lambda i, j: (i % pos_tiles, 0)),
        ],
        out_specs=pl.BlockSpec((tm, tn), lambda i, j: (i, jnp.maximum(j - 1, 0))),
        out_shape=jax.ShapeDtypeStruct((m, n), F32),
        scratch_shapes=[pltpu.VMEM((tm, d), BF16), pltpu.VMEM((tm, tn), F32),
                        pltpu.VMEM((tm, tn), F32)],
        compiler_params=_params("parallel", "arbitrary"),
        name="inproj",
    )(x, ln, w, cos, sin)


def _rope_tables(pos):
    half = HEAD_DIM // 2
    inv_freq = jnp.exp(-math.log(ROPE_THETA) * jnp.arange(half, dtype=F32) / half)
    ang = pos.astype(F32)[:, None] * inv_freq[None, :]
    cos, sin = jnp.cos(ang), jnp.sin(ang)
    return jnp.concatenate([cos, cos], axis=-1), jnp.concatenate([-sin, sin], axis=-1)


def _attn_prompt_body(q1, q2, q3, k1, k2, k3, v1, v2, v3,
                      attn_ref, kv128_ref, kv512_ref, kv2048_ref,
                      o_s, m_s, l_s):
    seq = q1.shape[0]
    blk = BAND_BLOCK
    for kv_ref, k_ref, v_ref in ((kv128_ref, k1, v1), (kv512_ref, k2, v2), (kv2048_ref, k3, v3)):
        keep = kv_ref.shape[1]
        kv_ref[0] = k_ref[pl.ds(seq - keep, keep), :]
        kv_ref[1] = v_ref[pl.ds(seq - keep, keep), :]

    row = lax.broadcasted_iota(jnp.int32, (blk, blk), 0)
    col = lax.broadcasted_iota(jnp.int32, (blk, blk), 1)
    own_mask = col <= row
    prev_mask = col >= row
    ones = jnp.ones((blk, HEAD_DIM), BF16)

    groups = ((q1, k1, v1), (q2, k2, v2), (q3, k3, v3))
    for g, ((q_ref, k_ref, v_ref), (_, dil)) in enumerate(zip(groups, ATTN_GROUPS)):
        nblk = seq // dil // blk
        has_prev = nblk > 1

        def rows(ref, start, dil=dil):
            if dil == 1:
                return ref[pl.ds(start, blk), :]
            return ref[pl.ds(start, blk, stride=dil), :]

        def sub(it, carry, g=g, dil=dil, nblk=nblk, has_prev=has_prev,
                q_ref=q_ref, k_ref=k_ref, v_ref=v_ref, rows=rows):
            work = []
            for u in range(PROMPT_BLOCKS_PER_TRIP):
                p = it * PROMPT_BLOCKS_PER_TRIP + u
                w = {"r": p // nblk, "n": p % nblk}
                w["qs"] = w["r"] + dil * (w["n"] * blk)
                w["q"] = rows(q_ref, w["qs"]).astype(BF16)
                w["kc"] = rows(k_ref, w["qs"]).astype(BF16)
                if has_prev:
                    w["ps"] = w["r"] + dil * (jnp.maximum(w["n"] - 1, 0) * blk)
                    w["kp"] = rows(k_ref, w["ps"]).astype(BF16)
                work.append(w)
            for w in work:
                w["s_own"] = jnp.where(own_mask, _dot_nt(w["q"], w["kc"]) * ATTN_SCALE, NEG_INF)
                if has_prev:
                    pm = jnp.logical_and(prev_mask, w["n"] > 0)
                    w["s_prev"] = jnp.where(pm, _dot_nt(w["q"], w["kp"]) * ATTN_SCALE, NEG_INF)
            for w in work:
                m = jnp.max(w["s_own"], axis=-1, keepdims=True)
                if has_prev:
                    m = jnp.maximum(m, jnp.max(w["s_prev"], axis=-1, keepdims=True))
                w["m"] = m
                w["e_own"] = jnp.exp(w["s_own"] - m).astype(BF16)
                if has_prev:
                    w["e_prev"] = jnp.exp(w["s_prev"] - m).astype(BF16)
            for w in work:
                acc = _dot(w["e_own"], rows(v_ref, w["qs"]).astype(BF16))
                l = _dot(w["e_own"], ones)
                if has_prev:
                    acc = acc + _dot(w["e_prev"], rows(v_ref, w["ps"]).astype(BF16))
                    l = l + _dot(w["e_prev"], ones)
                w["acc"] = acc
                w["l"] = l
            for w in work:
                if dil == 1:
                    idx = pl.ds(w["qs"], blk)
                else:
                    idx = pl.ds(w["qs"], blk, stride=dil)
                o_s[g, idx, :] = w["acc"]
                m_s[g, idx, :] = jnp.broadcast_to(w["m"], (blk, HEAD_DIM))
                l_s[g, idx, :] = w["l"]
            return carry

        lax.fori_loop(0, dil * nblk // PROMPT_BLOCKS_PER_TRIP, sub, 0)

    def combine(t, carry):
        idx = pl.ds(pl.multiple_of(t * blk, blk), blk)
        ms = [m_s[g, idx, :] for g in range(N_GROUPS)]
        mx = jnp.maximum(jnp.maximum(ms[0], ms[1]), ms[2])
        num = jnp.zeros((blk, HEAD_DIM), F32)
        den = jnp.zeros((blk, HEAD_DIM), F32)
        for g in range(N_GROUPS):
            w = jnp.exp(ms[g] - mx)
            num = num + w * o_s[g, idx, :]
            den = den + w * l_s[g, idx, :]
        attn_ref[idx, :] = (num / den).astype(attn_ref.dtype)
        return carry

    lax.fori_loop(0, seq // blk, combine, 0)


def _attn_prompt(z3, batch, seq):
    hd = HEAD_DIM
    nh = HEADS_PER_GROUP

    def col_spec(base):
        return pl.BlockSpec((None, seq, hd), lambda b, h, base=base: (b, 0, base + h))

    in_specs = [col_spec((t * N_GROUPS + g) * nh) for t in range(3) for g in range(N_GROUPS)]
    keeps = [min(w, seq) for w, _ in ATTN_GROUPS]
    kv_shapes = [jax.ShapeDtypeStruct((1, batch, 2, nh, keep, hd), F32) for keep in keeps]
    kv_specs = [pl.BlockSpec((None, None, 2, None, keep, hd), lambda b, h: (0, b, 0, h, 0, 0))
                for keep in keeps]
    return pl.pallas_call(
        _attn_prompt_body,
        grid=(batch, nh),
        in_specs=in_specs,
        out_specs=[pl.BlockSpec((None, seq, hd), lambda b, h: (b, 0, h))] + kv_specs,
        out_shape=[jax.ShapeDtypeStruct((batch, seq, nh * hd), BF16)] + kv_shapes,
        scratch_shapes=[pltpu.VMEM((N_GROUPS, seq, hd), F32) for _ in range(3)],
        compiler_params=_params("parallel", "parallel"),
        name="attn_prompt",
    )(*([z3] * 9))


def _attn_sample_body(q_ref, k_ref, v_ref, c1_ref, c2_ref, c3_hbm, attn_ref, c3_buf, c3_sem):
    nq = q_ref.shape[2]
    hd = HEAD_DIM
    pad = SUBLANES - nq

    b = pl.program_id(0)
    slot = b % 2

    def c3_copy(seq, s):
        return pltpu.make_async_copy(c3_hbm.at[0, seq, :, :, :, pl.ds(0, nq), :],
                                     c3_buf.at[s, :, :, :, pl.ds(0, nq), :], c3_sem.at[s])

    @pl.when(b == 0)
    def _():
        c3_buf[:, :, :, :, pl.ds(nq, pad), :] = jnp.zeros(c3_buf.shape[:4] + (pad, hd), F32)
        c3_copy(0, 0).start()

    @pl.when(b + 1 < pl.num_programs(0))
    def _():
        c3_copy(b + 1, 1 - slot).start()

    c3_copy(b, slot).wait()
    caches = (c1_ref, c2_ref, c3_buf.at[slot])

    def cache_rows(g, kv, h):
        x = caches[g][kv, h]
        return x.reshape(-1, hd).astype(BF16)

    def mask_for(g):
        n = caches[g].shape[2] * (caches[g].shape[3] if g == 2 else 1)
        row = lax.broadcasted_iota(jnp.int32, (SUBLANES, n), 0)
        col = lax.broadcasted_iota(jnp.int32, (SUBLANES, n), 1)
        if g == 0:
            return col >= row
        step = ATTN_GROUPS[1][1] if g == 1 else caches[g].shape[3]
        assert step & (step - 1) == 0
        return jnp.bitwise_and(col, step - 1) == row

    masks = [mask_for(g) for g in range(N_GROUPS)]
    qrow = lax.broadcasted_iota(jnp.int32, (SUBLANES, 1), 0)

    def scores(h):
        parts = []
        news = []
        for g in range(N_GROUPS):
            q = jnp.concatenate([q_ref[g, h], jnp.zeros((pad, hd), F32)], axis=0).astype(BF16)
            s = _dot_nt(q, cache_rows(g, 0, h)) * ATTN_SCALE
            parts.append((jnp.where(masks[g], s, NEG_INF), g))
            qf = q.astype(F32)
            kn = k_ref[g, h].astype(BF16).astype(F32)
            vn = v_ref[g, h].astype(BF16).astype(F32)
            if g == 0:
                for c in range(nq):
                    sn = jnp.sum(qf * kn[c:c + 1, :], axis=-1, keepdims=True) * ATTN_SCALE
                    news.append((jnp.where(qrow >= c, sn, NEG_INF), vn[c:c + 1, :]))
            else:
                knp = jnp.concatenate([kn, jnp.zeros((pad, hd), F32)], axis=0)
                sn = jnp.sum(qf * knp, axis=-1, keepdims=True) * ATTN_SCALE
                for c in range(nq):
                    news.append((jnp.where(qrow == c, sn, NEG_INF), vn[c:c + 1, :]))
        return parts, news

    def softmax(parts, news):
        m = parts[0][0].max(axis=-1, keepdims=True)
        for s, _ in parts[1:]:
            m = jnp.maximum(m, s.max(axis=-1, keepdims=True))
        for s, _ in news:
            m = jnp.maximum(m, s)
        es = [jnp.exp(s - m) for s, _ in parts]
        l = jnp.zeros((SUBLANES, 1), F32)
        for e in es:
            l = l + jnp.sum(e, axis=-1, keepdims=True)
        acc = jnp.zeros((SUBLANES, hd), F32)
        for s, v in news:
            e = jnp.exp(s - m)
            l = l + e
            acc = acc + e.astype(BF16).astype(F32) * v
        return es, l, acc

    def trip(it, carry):
        heads = [it * SAMPLE_HEADS_PER_TRIP + u for u in range(SAMPLE_HEADS_PER_TRIP)]
        staged = [scores(h) for h in heads]
        soft = [softmax(parts, news) for parts, news in staged]
        outs = []
        for h, (parts, _), (es, l, acc) in zip(heads, staged, soft):
            for e, (_, g) in zip(es, parts):
                acc = acc + _dot(e.astype(BF16), cache_rows(g, 1, h))
            outs.append((acc / l)[:nq, :])
        for h, out in zip(heads, outs):
            attn_ref[h] = out
        return carry

    lax.fori_loop(0, HEADS_PER_GROUP // SAMPLE_HEADS_PER_TRIP, trip, 0)


def _attn_sample(q_s, k_s, v_s, c128, c512, c2048):
    nb, _, nh, nq, hd = q_s.shape
    new_spec = pl.BlockSpec((None, N_GROUPS, nh, nq, hd), lambda b: (b, 0, 0, 0, 0))
    for cache, (window, _) in zip((c128, c512, c2048), ATTN_GROUPS):
        assert cache.shape[4] == window, "the caches must hold a full window"
    assert nq <= ATTN_GROUPS[1][1]
    dil3 = ATTN_GROUPS[2][1]
    rows3 = c2048.shape[4] // dil3
    c2048 = c2048.reshape(1, nb, 2, nh, rows3, dil3, hd)
    return pl.pallas_call(
        _attn_sample_body,
        grid=(nb,),
        in_specs=[
            new_spec, new_spec, new_spec,
            pl.BlockSpec((None, None, 2, nh, c128.shape[4], hd), lambda b: (0, b, 0, 0, 0, 0)),
            pl.BlockSpec((None, None, 2, nh, c512.shape[4], hd), lambda b: (0, b, 0, 0, 0, 0)),
            pl.BlockSpec(memory_space=pl.ANY),
        ],
        out_specs=pl.BlockSpec((None, nh, nq, hd), lambda b: (b, 0, 0, 0)),
        out_shape=jax.ShapeDtypeStruct((nb, nh, nq, hd), F32),
        scratch_shapes=[pltpu.VMEM((2, 2, nh, rows3, SUBLANES, hd), F32),
                        pltpu.SemaphoreType.DMA((2,))],
        compiler_params=_params("arbitrary"),
        name="attn_sample",
    )(q_s, k_s, v_s, c128, c512, c2048)


def _rglru_body(*refs, chained):
    if chained:
        (x_ref, y_ref, cw_ref, cb_ref, wri_ref, br_ref, bi_ref, lam_ref,
         rec_ref, hout_ref, tail_ref, sa, sb, carry_ref) = refs
    else:
        (x_ref, y_ref, h0_ref, cw_ref, cb_ref, wri_ref, br_ref, bi_ref, lam_ref,
         rec_ref, hout_ref) = refs
    tc, ch = x_ref.shape
    tiles = (tc // SUBLANES, SUBLANES, ch)
    step = pl.program_id(2) if chained else None
    row8 = lax.broadcasted_iota(jnp.int32, tiles, 1)

    cur = x_ref[...]
    if chained:
        @pl.when(step == 0)
        def _():
            tail_ref[...] = jnp.zeros_like(tail_ref)
            carry_ref[...] = jnp.zeros_like(carry_ref)

        prev = jnp.concatenate([tail_ref[...], cur[:tc - SUBLANES, :]], axis=0)
        tail_ref[...] = cur[tc - SUBLANES:, :]
    else:
        prev = cur
    cur = cur.reshape(tiles)
    prev = prev.reshape(tiles)

    xc = cb_ref[...]
    for j in range(CONV_W):
        d = CONV_W - 1 - j
        xs = cur if d == 0 else pltpu.roll(jnp.where(row8 >= SUBLANES - d, prev, cur), d, axis=1)
        xc = xc + xs * cw_ref[j:j + 1, :]

    xc2 = xc.reshape(tc, ch)
    pre_r, pre_i = [], []
    for n in range(ch // LRU_BLOCK_W):
        blk = xc2[:, n * LRU_BLOCK_W:(n + 1) * LRU_BLOCK_W].astype(BF16)
        ri = _dot(blk, wri_ref[n])
        pre_r.append(ri[:, :LRU_BLOCK_W])
        pre_i.append(ri[:, LRU_BLOCK_W:])
    r = jax.nn.sigmoid(jnp.concatenate(pre_r, axis=-1) + br_ref[...])
    i = jax.nn.sigmoid(jnp.concatenate(pre_i, axis=-1) + bi_ref[...])
    log_a = r * (-LRU_C * jax.nn.softplus(-lam_ref[...]))
    a = jnp.exp(log_a)
    th = jnp.tanh(log_a)
    p = -2.0 * th
    w = 1.0 - th
    mult = jnp.where(p > 0.0, p * lax.rsqrt(p * w), 0.0)
    b = mult * (i * xc2)

    a = a.reshape(tiles)
    b = b.reshape(tiles)
    if not chained:
        a = jnp.where(row8 < CONV_W, 1.0, a)
        b = jnp.where(row8 < CONV_W, h0_ref[...].reshape(tiles), b)
    shift = 1
    while shift < SUBLANES:
        valid = row8 >= shift
        a_prev = pltpu.roll(a, shift, axis=1)
        b_prev = pltpu.roll(b, shift, axis=1)
        b = jnp.where(valid, a * b_prev + b, b)
        a = jnp.where(valid, a * a_prev, a)
        shift *= 2
    a = a.reshape(tc, ch)
    b = b.reshape(tc, ch)

    if chained:
        sa[...] = a
        sb[...] = b

        def link(k, carry):
            idx = pl.ds(pl.multiple_of(k * SUBLANES, SUBLANES), SUBLANES)
            h = sa[idx, :] * carry + sb[idx, :]
            sb[idx, :] = h
            return jnp.broadcast_to(h[SUBLANES - 1:SUBLANES, :], (SUBLANES, ch))

        carry = lax.fori_loop(0, tc // SUBLANES, link, carry_ref[...])
        carry_ref[...] = carry
        h = sb[...]
    else:
        h = b

    rec_ref[...] = (h * jax.nn.gelu(y_ref[...])).astype(rec_ref.dtype)
    if chained:
        @pl.when(step == pl.num_programs(2) - 1)
        def _():
            hout_ref[...] = h[tc - 1:tc, :]
    else:
        hout_ref[...] = h


def _rglru_params_specs(ch, idx):
    nblk = ch // LRU_BLOCK_W
    return [
        pl.BlockSpec((CONV_W, ch), lambda *g: (0, idx(*g))),
        pl.BlockSpec((1, ch), lambda *g: (0, idx(*g))),
        pl.BlockSpec((nblk, LRU_BLOCK_W, 2 * LRU_BLOCK_W), lambda *g: (idx(*g), 0, 0)),
        pl.BlockSpec((1, ch), lambda *g: (0, idx(*g))),
        pl.BlockSpec((1, ch), lambda *g: (0, idx(*g))),
        pl.BlockSpec((1, ch), lambda *g: (0, idx(*g))),
    ]


def _rglru_prompt(z3, x_col, y_col, params, *, d_rnn, tc, ch):
    batch, seq, _ = z3.shape
    nc = d_rnn // ch
    rec, hlast = pl.pallas_call(
        functools.partial(_rglru_body, chained=True),
        grid=(batch, nc, seq // tc),
        in_specs=[
            pl.BlockSpec((None, tc, ch), lambda b, c, t: (b, t, x_col // ch + c)),
            pl.BlockSpec((None, tc, ch), lambda b, c, t: (b, t, y_col // ch + c)),
        ] + _rglru_params_specs(ch, lambda b, c, t: c),
        out_specs=[
            pl.BlockSpec((None, tc, ch), lambda b, c, t: (b, t, c)),
            pl.BlockSpec((None, 1, ch), lambda b, c, t: (b, 0, c)),
        ],
        out_shape=[jax.ShapeDtypeStruct((batch, seq, d_rnn), BF16),
                   jax.ShapeDtypeStruct((batch, 1, d_rnn), F32)],
        scratch_shapes=[pltpu.VMEM((SUBLANES, ch), F32),
                        pltpu.VMEM((tc, ch), F32), pltpu.VMEM((tc, ch), F32),
                        pltpu.VMEM((SUBLANES, ch), F32)],
        compiler_params=_params("parallel", "parallel", "arbitrary"),
        name="rglru_prompt",
    )(z3, z3, *params)
    return rec, hlast


def _rglru_sample(x8, y8, h8, params, *, tc, ch):
    rows, d_rnn = x8.shape
    nc = d_rnn // ch
    row_spec = pl.BlockSpec((tc, ch), lambda c, t: (t, c))
    return pl.pallas_call(
        functools.partial(_rglru_body, chained=False),
        grid=(nc, rows // tc),
        in_specs=[row_spec, row_spec, row_spec] + _rglru_params_specs(ch, lambda c, t: c),
        out_specs=[row_spec, row_spec],
        out_shape=[jax.ShapeDtypeStruct((rows, d_rnn), F32),
                   jax.ShapeDtypeStruct((rows, d_rnn), F32)],
        compiler_params=_params("parallel", "parallel"),
        name="rglru_sample",
    )(x8, y8, h8, *params)


def _merge_body(x_ref, attn_ref, rec_ref, ga0_ref, ga1_ref, gr0_ref, gr1_ref,
                wba_ref, wbr_ref, wout_ref, o_ref):
    pa = _dot(attn_ref[...].astype(BF16), wba_ref[...])
    pr = _dot(rec_ref[...].astype(BF16), wbr_ref[...])
    ga = jnp.concatenate([ga0_ref[...], ga1_ref[...]], axis=-1)
    gr = jnp.concatenate([gr0_ref[...], gr1_ref[...]], axis=-1)
    merged = jax.nn.sigmoid(ga) * pa + jax.nn.sigmoid(gr) * pr
    o_ref[...] = x_ref[...] + _dot(merged.astype(BF16), wout_ref[...])


def _merge(x, attn, rec, z, g_col, wba, wbr, wout, *, tm):
    m, d = x.shape
    half = d // 2
    gb = g_col // half

    def whole(w):
        return pl.BlockSpec(w.shape, lambda i: (0, 0), pipeline_mode=pl.Buffered(1))

    def gate_spec(k):
        return pl.BlockSpec((tm, half), lambda i, k=k: (i, gb + k))

    return pl.pallas_call(
        _merge_body,
        grid=(m // tm,),
        in_specs=[
            pl.BlockSpec((tm, d), lambda i: (i, 0)),
            pl.BlockSpec((tm, attn.shape[1]), lambda i: (i, 0)),
            pl.BlockSpec((tm, rec.shape[1]), lambda i: (i, 0)),
            gate_spec(0), gate_spec(1), gate_spec(2), gate_spec(3),
            whole(wba), whole(wbr), whole(wout),
        ],
        out_specs=pl.BlockSpec((tm, d), lambda i: (i, 0)),
        out_shape=jax.ShapeDtypeStruct((m, d), F32),
        compiler_params=_params("parallel"),
        name="merge",
    )(x, attn, rec, z, z, z, z, wba, wbr, wout)


def _ple_body(x_ref, pe_ref, lnp_ref, wpg_ref, wpp_ref, lnf_ref, y_ref):
    x = x_ref[...]
    hn = _rmsnorm(x, lnp_ref[...]).astype(BF16)
    gate = jax.nn.sigmoid(_dot(hn, wpg_ref[...]))
    proj = _dot(pe_ref[...].astype(BF16), wpp_ref[...])
    y_ref[...] = _rmsnorm(x + gate * proj, lnf_ref[...])


def _ple(x, pe, lnp, wpg, wpp, lnf, *, tm):
    m, d = x.shape

    def whole(w):
        return pl.BlockSpec(w.shape, lambda i: (0, 0), pipeline_mode=pl.Buffered(1))

    return pl.pallas_call(
        _ple_body,
        grid=(m // tm,),
        in_specs=[
            pl.BlockSpec((tm, d), lambda i: (i, 0)),
            pl.BlockSpec((tm, pe.shape[1]), lambda i: (i, 0)),
            pl.BlockSpec((1, d), lambda i: (0, 0)),
            whole(wpg), whole(wpp),
            pl.BlockSpec((1, d), lambda i: (0, 0)),
        ],
        out_specs=pl.BlockSpec((tm, d), lambda i: (i, 0)),
        out_shape=jax.ShapeDtypeStruct((m, d), F32),
        compiler_params=_params("parallel"),
        name="ple",
    )(x, pe, lnp, wpg, wpp, lnf)


def kernel(x_prompt, x_sample, cache_kv_w128, cache_kv_w512, cache_kv_w2048, state_conv, state_rglru, p_prompt, p_sample, ln_ffn1, w_ffn1_gate, w_ffn1_up, w_ffn1_down, ln_mix, w_in, conv_w, conv_b, w_rgate, b_rgate, w_igate, b_igate, lru_lambda, w_branch_attn, w_branch_rec, w_out, ln_ffn2, w_ffn2_gate, w_ffn2_up, w_ffn2_down, ln_ple, w_ple_gate, w_ple_proj, ln_final):
    batch, seq, d = x_prompt.shape
    nb, nq, _ = x_sample.shape
    past_len = cache_kv_w2048.shape[4]
    d_rnn = conv_w.shape[-1]
    depth = w_in.shape[0]
    assert depth == 1

    xp = x_prompt.reshape(batch * seq, d)
    xs = x_sample.reshape(nb * nq, d)
    cos_p, sin_p = _rope_tables(jnp.arange(seq))
    cos_s, sin_s = _rope_tables(past_len + (jnp.arange(nb * nq) % nq))

    l = 0
    row = lambda v: v[l].reshape(1, -1)
    wri = jnp.concatenate([w_rgate[l], w_igate[l]], axis=-1).astype(BF16)
    rg_params = (conv_w[l], row(conv_b), wri, row(b_rgate), row(b_igate), row(lru_lambda))
    w1 = (w_ffn1_gate, w_ffn1_up, w_ffn1_down)
    w2 = (w_ffn2_gate, w_ffn2_up, w_ffn2_down)
    ffn_tiles = dict(tm=1024, tf=256)
    win = w_in[l].astype(BF16)
    wba = w_branch_attn[l].astype(BF16)
    wbr = w_branch_rec[l].astype(BF16)
    wout = w_out[l].astype(BF16)
    wpg = w_ple_gate[l].astype(BF16)
    wpp = w_ple_proj[l].astype(BF16)
    lnf = ln_final.reshape(1, -1)

    x_col = 3 * QKV_WIDTH
    y_col = x_col + d_rnn
    g_col = y_col + d_rnn
    tm = 512

    xp = _ffn(xp, row(ln_ffn1), *w1, **ffn_tiles)
    zp = _inproj(xp, row(ln_mix), win, cos_p, sin_p, tm=1024, tn=1024)
    zp3 = zp.reshape(batch, seq, -1)
    attn_p, kv1_p, kv2_p, kv3_p = _attn_prompt(zp3, batch, seq)
    rec_p, hlast_p = _rglru_prompt(zp3, x_col, y_col, rg_params, d_rnn=d_rnn, tc=256, ch=1024)
    conv_p = zp3[:, seq - (CONV_W - 1):, x_col:x_col + d_rnn][None]
    xp = _merge(xp, attn_p.reshape(batch * seq, -1), rec_p.reshape(batch * seq, -1), zp, g_col,
                wba, wbr, wout, tm=256)
    xp = _ffn(xp, row(ln_ffn2), *w2, **ffn_tiles)
    yp = _ple(xp, p_prompt[l].reshape(batch * seq, -1), row(ln_ple), wpg, wpp, lnf, tm=256)

    xs = _ffn(xs, row(ln_ffn1), *w1, **ffn_tiles)
    zs = _inproj(xs, row(ln_mix), win, cos_s, sin_s, tm=tm, tn=512)
    zs3 = zs.reshape(nb, nq, -1)
    qkv = zs3[:, :, :3 * QKV_WIDTH].reshape(nb, nq, 3, N_GROUPS, HEADS_PER_GROUP, HEAD_DIM)
    qkv = qkv.transpose(2, 0, 3, 4, 1, 5)
    attn_s = _attn_sample(qkv[0], qkv[1], qkv[2], cache_kv_w128, cache_kv_w512, cache_kv_w2048)
    attn_s = attn_s.transpose(0, 2, 1, 3).reshape(nb * nq, -1)
    kv_s = [jnp.stack([qkv[1][:, g], qkv[2][:, g]], axis=1)[None] for g in range(N_GROUPS)]

    x_rec = zs3[:, :, x_col:x_col + d_rnn]
    y_rec = zs3[:, :, y_col:y_col + d_rnn]
    zero1 = jnp.zeros((nb, 1, d_rnn), F32)
    x8 = jnp.concatenate([zero1, state_conv[l], x_rec], axis=1).reshape(nb * SUBLANES, d_rnn)
    y8 = jnp.concatenate([jnp.zeros((nb, CONV_W, d_rnn), F32), y_rec], axis=1).reshape(nb * SUBLANES, d_rnn)
    h8 = jnp.concatenate([jnp.zeros((nb, CONV_W - 1, d_rnn), F32), state_rglru[l][:, None],
                          jnp.zeros((nb, nq, d_rnn), F32)], axis=1).reshape(nb * SUBLANES, d_rnn)
    rec8, hs8 = _rglru_sample(x8, y8, h8, rg_params, tc=256, ch=1024)
    rec_s = rec8.reshape(nb, SUBLANES, d_rnn)[:, CONV_W:].reshape(nb * nq, d_rnn)
    hlast_s = hs8.reshape(nb, SUBLANES, d_rnn)[:, SUBLANES - 1][None]
    conv_s = jnp.concatenate([state_conv[l], x_rec], axis=1)[:, -(CONV_W - 1):][None]
    xs = _merge(xs, attn_s, rec_s, zs, g_col, wba, wbr, wout, tm=256)
    xs = _ffn(xs, row(ln_ffn2), *w2, **ffn_tiles)
    ys = _ple(xs, p_sample[l].reshape(nb * nq, -1), row(ln_ple), wpg, wpp, lnf, tm=256)

    return (yp.reshape(batch, seq, d), ys.reshape(nb, nq, d),
            kv1_p, kv2_p, kv3_p, conv_p, hlast_p.reshape(1, batch, d_rnn),
            kv_s[0], kv_s[1], kv_s[2], conv_s, hlast_s)
```

```python
import functools
import math

import jax
import jax.numpy as jnp
from jax import lax
from jax.experimental import pallas as pl
from jax.experimental.pallas import tpu as pltpu

F32 = jnp.float32
BF16 = jnp.bfloat16

HEAD_DIM = 128
HEADS_PER_GROUP = 8
ATTN_GROUPS = ((128, 1), (512, 4), (2048, 16))
N_GROUPS = 3
QKV_WIDTH = N_GROUPS * HEADS_PER_GROUP * HEAD_DIM
ATTN_SCALE = HEAD_DIM ** -0.5
BAND_BLOCK = 128
ROPE_THETA = 10000.0
LRU_BLOCK_W = 128
LRU_C = 8.0
CONV_W = 4
EPS = 1e-6

SUBLANES = 8
LANES = 128
VMEM_LIMIT = 48 * 1024 * 1024
FFN_VMEM_LIMIT = 56 * 1024 * 1024
NEG_INF = float("-inf")
PROMPT_BLOCKS_PER_TRIP = 16
SAMPLE_HEADS_PER_TRIP = 4
ROPE_ROW_BAND = 64
SCAN_TILES_PER_PASS = 16


def _params(*semantics):
    return pltpu.CompilerParams(dimension_semantics=semantics, vmem_limit_bytes=VMEM_LIMIT)


def _rmsnorm(x, g):
    return x * lax.rsqrt(jnp.mean(x * x, axis=-1, keepdims=True) + EPS) * g


def _dot(a, b):
    return jnp.dot(a, b, preferred_element_type=F32)


def _dot_nt(a, b):
    return lax.dot_general(a, b, (((1,), (1,)), ((), ())), preferred_element_type=F32)


def _ffn_body(x_ref, ln_ref, wg_ref, wu_ref, wd_ref, o_ref, h_ref):
    f = pl.program_id(1)

    @pl.when(f == 0)
    def _():
        x = x_ref[...]
        h_ref[...] = _rmsnorm(x, ln_ref[...]).astype(BF16)
        o_ref[...] = x

    h = h_ref[...]
    g = _dot(h, wg_ref[...].astype(BF16))
    u = _dot(h, wu_ref[...].astype(BF16))
    a = (0.5 * g * jax.nn.sigmoid(g)) * u
    o_ref[...] += _dot(a.astype(BF16), wd_ref[...].astype(BF16))


def _ffn(x, ln, wg, wu, wd, *, tm, tf):
    m, d = x.shape
    tm = min(tm, m)
    ff = wg.shape[2]
    return pl.pallas_call(
        _ffn_body,
        grid=(m // tm, ff // tf),
        in_specs=[
            pl.BlockSpec((tm, d), lambda i, f: (i, 0)),
            pl.BlockSpec((1, d), lambda i, f: (0, 0)),
            pl.BlockSpec((None, d, tf), lambda i, f: (0, 0, f)),
            pl.BlockSpec((None, d, tf), lambda i, f: (0, 0, f)),
            pl.BlockSpec((None, tf, d), lambda i, f: (0, f, 0)),
        ],
        out_specs=pl.BlockSpec((tm, d), lambda i, f: (i, 0)),
        out_shape=jax.ShapeDtypeStruct((m, d), F32),
        scratch_shapes=[pltpu.VMEM((tm, d), BF16)],
        compiler_params=pltpu.CompilerParams(dimension_semantics=("parallel", "arbitrary"),
                                             vmem_limit_bytes=FFN_VMEM_LIMIT),
        name="ffn",
    )(x, ln, wg, wu, wd)


def _inproj_body(x_ref, ln_ref, w_ref, cos_ref, sin_ref, z_ref, h_ref, zb0, zb1, *,
                 rope_tiles, col_tiles):
    j = pl.program_id(1)
    bufs = (zb0, zb1)
    assert 0 < rope_tiles < col_tiles

    def finish_rope(src):
        for r0 in range(0, z_ref.shape[0], ROPE_ROW_BAND):
            rows = slice(r0, r0 + ROPE_ROW_BAND)
            cos = cos_ref[rows, :]
            sin = sin_ref[rows, :]
            for c in range(z_ref.shape[1] // HEAD_DIM):
                cols = slice(c * HEAD_DIM, (c + 1) * HEAD_DIM)
                zc = src[rows, cols]
                z_ref[rows, cols] = zc * cos + pltpu.roll(zc, HEAD_DIM // 2, axis=1) * sin

    @pl.when(j == 0)
    def _():
        h_ref[...] = _rmsnorm(x_ref[...], ln_ref[...]).astype(BF16)
        zb0[...] = _dot(h_ref[...], w_ref[...])

    for parity in (0, 1):
        cur, prev = bufs[parity], bufs[1 - parity]
        mine = (j % 2) == parity

        @pl.when(mine & (j >= 1) & (j < rope_tiles))
        def _(cur=cur, prev=prev):
            cur[...] = _dot(h_ref[...], w_ref[...])
            finish_rope(prev)

    @pl.when(j == rope_tiles)
    def _():
        finish_rope(bufs[(rope_tiles - 1) % 2])

    @pl.when(j > rope_tiles)
    def _():
        z_ref[...] = _dot(h_ref[...], w_ref[...])


def _inproj(x, ln, w, cos, sin, *, tm, tn):
    m, d = x.shape
    tm = min(tm, m)
    n = w.shape[1]
    col_tiles = n // tn
    rope_tiles = 2 * QKV_WIDTH // tn
    pos_tiles = cos.shape[0] // tm
    return pl.pallas_call(
        functools.partial(_inproj_body, rope_tiles=rope_tiles, col_tiles=col_tiles),
        grid=(m // tm, col_tiles + 1),
        in_specs=[
            pl.BlockSpec((tm, d), lambda i, j: (i, 0)),
            pl.BlockSpec((1, d), lambda i, j: (0, 0)),
            pl.BlockSpec((d, tn), lambda i, j: (0, jnp.where(j < rope_tiles, j, j - 1))),
            pl.BlockSpec((tm, HEAD_DIM), lambda i, j: (i % pos_tiles, 0)),
            pl.BlockSpec((tm, HEAD_DIM), lambda i, j: (i % pos_tiles, 0)),
        ],
        out_specs=pl.BlockSpec((tm, tn), lambda i, j: (i, jnp.maximum(j - 1, 0))),
        out_shape=jax.ShapeDtypeStruct((m, n), F32),
        scratch_shapes=[pltpu.VMEM((tm, d), BF16), pltpu.VMEM((tm, tn), F32),
                        pltpu.VMEM((tm, tn), F32)],
        compiler_params=_params("parallel", "arbitrary"),
        name="inproj",
    )(x, ln, w, cos, sin)


def _rope_tables(pos):
    half = HEAD_DIM // 2
    inv_freq = jnp.exp(-math.log(ROPE_THETA) * jnp.arange(half, dtype=F32) / half)
    ang = pos.astype(F32)[:, None] * inv_freq[None, :]
    cos, sin = jnp.cos(ang), jnp.sin(ang)
    return jnp.concatenate([cos, cos], axis=-1), jnp.concatenate([-sin, sin], axis=-1)


def _attn_prompt_body(q1, q2, q3, k1, k2, k3, v1, v2, v3,
                      attn_ref, kv128_ref, kv512_ref, kv2048_ref,
                      o_s, m_s, l_s):
    seq = q1.shape[0]
    blk = BAND_BLOCK
    for kv_ref, k_ref, v_ref in ((kv128_ref, k1, v1), (kv512_ref, k2, v2), (kv2048_ref, k3, v3)):
        keep = kv_ref.shape[1]
        kv_ref[0] = k_ref[pl.ds(seq - keep, keep), :]
        kv_ref[1] = v_ref[pl.ds(seq - keep, keep), :]

    row = lax.broadcasted_iota(jnp.int32, (blk, blk), 0)
    col = lax.broadcasted_iota(jnp.int32, (blk, blk), 1)
    own_mask = col <= row
    prev_mask = col >= row
    ones = jnp.ones((blk, HEAD_DIM), BF16)

    groups = ((q1, k1, v1), (q2, k2, v2), (q3, k3, v3))
    for g, ((q_ref, k_ref, v_ref), (_, dil)) in enumerate(zip(groups, ATTN_GROUPS)):
        nblk = seq // dil // blk
        has_prev = nblk > 1

        def rows(ref, start, dil=dil):
            if dil == 1:
                return ref[pl.ds(start, blk), :]
            return ref[pl.ds(start, blk, stride=dil), :]

        def sub(it, carry, g=g, dil=dil, nblk=nblk, has_prev=has_prev,
                q_ref=q_ref, k_ref=k_ref, v_ref=v_ref, rows=rows):
            work = []
            for u in range(PROMPT_BLOCKS_PER_TRIP):
                p = it * PROMPT_BLOCKS_PER_TRIP + u
                w = {"r": p // nblk, "n": p % nblk}
                w["qs"] = w["r"] + dil * (w["n"] * blk)
                w["q"] = rows(q_ref, w["qs"]).astype(BF16)
                w["kc"] = rows(k_ref, w["qs"]).astype(BF16)
                if has_prev:
                    w["ps"] = w["r"] + dil * (jnp.maximum(w["n"] - 1, 0) * blk)
                    w["kp"] = rows(k_ref, w["ps"]).astype(BF16)
                work.append(w)
            for w in work:
                w["s_own"] = jnp.where(own_mask, _dot_nt(w["q"], w["kc"]) * ATTN_SCALE, NEG_INF)
                if has_prev:
                    pm = jnp.logical_and(prev_mask, w["n"] > 0)
                    w["s_prev"] = jnp.where(pm, _dot_nt(w["q"], w["kp"]) * ATTN_SCALE, NEG_INF)
            for w in work:
                m = jnp.max(w["s_own"], axis=-1, keepdims=True)
                if has_prev:
                    m = jnp.maximum(m, jnp.max(w["s_prev"], axis=-1, keepdims=True))
                w["m"] = m
                w["e_own"] = jnp.exp(w["s_own"] - m).astype(BF16)
                if has_prev:
                    w["e_prev"] = jnp.exp(w["s_prev"] - m).astype(BF16)
            for w in work:
                acc = _dot(w["e_own"], rows(v_ref, w["qs"]).astype(BF16))
                l = _dot(w["e_own"], ones)
                if has_prev:
                    acc = acc + _dot(w["e_prev"], rows(v_ref, w["ps"]).astype(BF16))
                    l = l + _dot(w["e_prev"], ones)
                w["acc"] = acc
                w["l"] = l
            for w in work:
                if dil == 1:
                    idx = pl.ds(w["qs"], blk)
                else:
                    idx = pl.ds(w["qs"], blk, stride=dil)
                o_s[g, idx, :] = w["acc"]
                m_s[g, idx, :] = jnp.broadcast_to(w["m"], (blk, HEAD_DIM))
                l_s[g, idx, :] = w["l"]
            return carry

        lax.fori_loop(0, dil * nblk // PROMPT_BLOCKS_PER_TRIP, sub, 0)

    def combine(t, carry):
        idx = pl.ds(pl.multiple_of(t * blk, blk), blk)
        ms = [m_s[g, idx, :] for g in range(N_GROUPS)]
        mx = jnp.maximum(jnp.maximum(ms[0], ms[1]), ms[2])
        num = jnp.zeros((blk, HEAD_DIM), F32)
        den = jnp.zeros((blk, HEAD_DIM), F32)
        for g in range(N_GROUPS):
            w = jnp.exp(ms[g] - mx)
            num = num + w * o_s[g, idx, :]
            den = den + w * l_s[g, idx, :]
        attn_ref[idx, :] = (num / den).astype(attn_ref.dtype)
        return carry

    lax.fori_loop(0, seq // blk, combine, 0)


def _attn_prompt(z3, batch, seq):
    hd = HEAD_DIM
    nh = HEADS_PER_GROUP

    def col_spec(base):
        return pl.BlockSpec((None, seq, hd), lambda b, h, base=base: (b, 0, base + h))

    in_specs = [col_spec((t * N_GROUPS + g) * nh) for t in range(3) for g in range(N_GROUPS)]
    keeps = [min(w, seq) for w, _ in ATTN_GROUPS]
    kv_shapes = [jax.ShapeDtypeStruct((1, batch, 2, nh, keep, hd), F32) for keep in keeps]
    kv_specs = [pl.BlockSpec((None, None, 2, None, keep, hd), lambda b, h: (0, b, 0, h, 0, 0))
                for keep in keeps]
    return pl.pallas_call(
        _attn_prompt_body,
        grid=(batch, nh),
        in_specs=in_specs,
        out_specs=[pl.BlockSpec((None, seq, hd), lambda b, h: (b, 0, h))] + kv_specs,
        out_shape=[jax.ShapeDtypeStruct((batch, seq, nh * hd), BF16)] + kv_shapes,
        scratch_shapes=[pltpu.VMEM((N_GROUPS, seq, hd), F32) for _ in range(3)],
        compiler_params=_params("parallel", "parallel"),
        name="attn_prompt",
    )(*([z3] * 9))


def _attn_sample_body(q_ref, k_ref, v_ref, c1_ref, c2_ref, c3_hbm, attn_ref, c3_buf, c3_sem):
    nq = q_ref.shape[2]
    hd = HEAD_DIM
    pad = SUBLANES - nq

    b = pl.program_id(0)
    slot = b % 2

    def c3_copy(seq, s):
        return pltpu.make_async_copy(c3_hbm.at[0, seq, :, :, :, pl.ds(0, nq), :],
                                     c3_buf.at[s, :, :, :, pl.ds(0, nq), :], c3_sem.at[s])

    @pl.when(b == 0)
    def _():
        c3_buf[:, :, :, :, pl.ds(nq, pad), :] = jnp.zeros(c3_buf.shape[:4] + (pad, hd), F32)
        c3_copy(0, 0).start()

    @pl.when(b + 1 < pl.num_programs(0))
    def _():
        c3_copy(b + 1, 1 - slot).start()

    c3_copy(b, slot).wait()
    caches = (c1_ref, c2_ref, c3_buf.at[slot])

    def cache_rows(g, kv, h):
        x = caches[g][kv, h]
        return x.reshape(-1, hd).astype(BF16)

    def mask_for(g):
        n = caches[g].shape[2] * (caches[g].shape[3] if g == 2 else 1)
        row = lax.broadcasted_iota(jnp.int32, (SUBLANES, n), 0)
        col = lax.broadcasted_iota(jnp.int32, (SUBLANES, n), 1)
        if g == 0:
            return col >= row
        step = ATTN_GROUPS[1][1] if g == 1 else caches[g].shape[3]
        assert step & (step - 1) == 0
        return jnp.bitwise_and(col, step - 1) == row

    masks = [mask_for(g) for g in range(N_GROUPS)]
    qrow = lax.broadcasted_iota(jnp.int32, (SUBLANES, 1), 0)

    def scores(h):
        parts = []
        news = []
        for g in range(N_GROUPS):
            q = jnp.concatenate([q_ref[g, h], jnp.zeros((pad, hd), F32)], axis=0).astype(BF16)
            s = _dot_nt(q, cache_rows(g, 0, h)) * ATTN_SCALE
            parts.append((jnp.where(masks[g], s, NEG_INF), g))
            qf = q.astype(F32)
            kn = k_ref[g, h].astype(BF16).astype(F32)
            vn = v_ref[g, h].astype(BF16).astype(F32)
            if g == 0:
                for c in range(nq):
                    sn = jnp.sum(qf * kn[c:c + 1, :], axis=-1, keepdims=True) * ATTN_SCALE
                    news.append((jnp.where(qrow >= c, sn, NEG_INF), vn[c:c + 1, :]))
            else:
                knp = jnp.concatenate([kn, jnp.zeros((pad, hd), F32)], axis=0)
                sn = jnp.sum(qf * knp, axis=-1, keepdims=True) * ATTN_SCALE
                for c in range(nq):
                    news.append((jnp.where(qrow == c, sn, NEG_INF), vn[c:c + 1, :]))
        return parts, news

    def softmax(parts, news):
        m = parts[0][0].max(axis=-1, keepdims=True)
        for s, _ in parts[1:]:
            m = jnp.maximum(m, s.max(axis=-1, keepdims=True))
        for s, _ in news:
            m = jnp.maximum(m, s)
        es = [jnp.exp(s - m) for s, _ in parts]
        l = jnp.zeros((SUBLANES, 1), F32)
        for e in es:
            l = l + jnp.sum(e, axis=-1, keepdims=True)
        acc = jnp.zeros((SUBLANES, hd), F32)
        for s, v in news:
            e = jnp.exp(s - m)
            l = l + e
            acc = acc + e.astype(BF16).astype(F32) * v
        return es, l, acc

    def trip(it, carry):
        heads = [it * SAMPLE_HEADS_PER_TRIP + u for u in range(SAMPLE_HEADS_PER_TRIP)]
        staged = [scores(h) for h in heads]
        soft = [softmax(parts, news) for parts, news in staged]
        outs = []
        for h, (parts, _), (es, l, acc) in zip(heads, staged, soft):
            for e, (_, g) in zip(es, parts):
                acc = acc + _dot(e.astype(BF16), cache_rows(g, 1, h))
            outs.append((acc / l)[:nq, :])
        for h, out in zip(heads, outs):
            attn_ref[h] = out
        return carry

    lax.fori_loop(0, HEADS_PER_GROUP // SAMPLE_HEADS_PER_TRIP, trip, 0)


def _attn_sample(q_s, k_s, v_s, c128, c512, c2048):
    nb, _, nh, nq, hd = q_s.shape
    new_spec = pl.BlockSpec((None, N_GROUPS, nh, nq, hd), lambda b: (b, 0, 0, 0, 0))
    for cache, (window, _) in zip((c128, c512, c2048), ATTN_GROUPS):
        assert cache.shape[4] == window, "the caches must hold a full window"
    assert nq <= ATTN_GROUPS[1][1]
    dil3 = ATTN_GROUPS[2][1]
    rows3 = c2048.shape[4] // dil3
    c2048 = c2048.reshape(1, nb, 2, nh, rows3, dil3, hd)
    return pl.pallas_call(
        _attn_sample_body,
        grid=(nb,),
        in_specs=[
            new_spec, new_spec, new_spec,
            pl.BlockSpec((None, None, 2, nh, c128.shape[4], hd), lambda b: (0, b, 0, 0, 0, 0)),
            pl.BlockSpec((None, None, 2, nh, c512.shape[4], hd), lambda b: (0, b, 0, 0, 0, 0)),
            pl.BlockSpec(memory_space=pl.ANY),
        ],
        out_specs=pl.BlockSpec((None, nh, nq, hd), lambda b: (b, 0, 0, 0)),
        out_shape=jax.ShapeDtypeStruct((nb, nh, nq, hd), F32),
        scratch_shapes=[pltpu.VMEM((2, 2, nh, rows3, SUBLANES, hd), F32),
                        pltpu.SemaphoreType.DMA((2,))],
        compiler_params=_params("arbitrary"),
        name="attn_sample",
    )(q_s, k_s, v_s, c128, c512, c2048)


def _rglru_body(*refs, chained):
    if chained:
        (x_ref, y_ref, cw_ref, cb_ref, wri_ref, br_ref, bi_ref, lam_ref,
         rec_ref, hout_ref, sa, sb, sh, tail_ref, carry_ref) = refs
    else:
        (x_ref, y_ref, h0_ref, cw_ref, cb_ref, wri_ref, br_ref, bi_ref, lam_ref,
         rec_ref, hout_ref, sa, sb, sh) = refs
    tc, ch = x_ref.shape
    tiles = (tc // SUBLANES, SUBLANES, ch)
    step = pl.program_id(2) if chained else None
    row8 = lax.broadcasted_iota(jnp.int32, tiles, 1)

    cur = x_ref[...]
    if chained:
        @pl.when(step == 0)
        def _():
            tail_ref[...] = jnp.zeros_like(tail_ref)
            carry_ref[...] = jnp.zeros_like(carry_ref)

        prev = jnp.concatenate([tail_ref[...], cur[:tc - SUBLANES, :]], axis=0)
        tail_ref[...] = cur[tc - SUBLANES:, :]
    else:
        prev = cur
    cur = cur.reshape(tiles)
    prev = prev.reshape(tiles)

    xc = cb_ref[...]
    for j in range(CONV_W):
        d = CONV_W - 1 - j
        xs = cur if d == 0 else pltpu.roll(jnp.where(row8 >= SUBLANES - d, prev, cur), d, axis=1)
        xc = xc + xs * cw_ref[j:j + 1, :]

    xc2 = xc.reshape(tc, ch)
    pre_r, pre_i = [], []
    for n in range(ch // LRU_BLOCK_W):
        blk = xc2[:, n * LRU_BLOCK_W:(n + 1) * LRU_BLOCK_W].astype(BF16)
        ri = _dot(blk, wri_ref[n])
        pre_r.append(ri[:, :LRU_BLOCK_W])
        pre_i.append(ri[:, LRU_BLOCK_W:])
    r = jax.nn.sigmoid(jnp.concatenate(pre_r, axis=-1) + br_ref[...])
    i = jax.nn.sigmoid(jnp.concatenate(pre_i, axis=-1) + bi_ref[...])
    log_a = r * (-LRU_C * jax.nn.softplus(-lam_ref[...]))
    a = jnp.exp(log_a)
    th = jnp.tanh(log_a)
    p = -2.0 * th
    w = 1.0 - th
    mult = jnp.where(p > 0.0, p * lax.rsqrt(p * w), 0.0)
    b = mult * (i * xc2)

    if not chained:
        a = jnp.where(row8 < CONV_W, 1.0, a.reshape(tiles)).reshape(tc, ch)
        b = jnp.where(row8 < CONV_W, h0_ref[...].reshape(tiles), b.reshape(tiles)).reshape(tc, ch)

    slabs = ch // LANES
    for c in range(slabs):
        sa[c] = a[:, c * LANES:(c + 1) * LANES]
        sb[c] = b[:, c * LANES:(c + 1) * LANES]
    span = SCAN_TILES_PER_PASS
    tile_row = lax.broadcasted_iota(jnp.int32, (span, LANES), 0)
    for c in range(slabs):
        carry = carry_ref[c, 0:1, :] if chained else None
        for t0 in range(0, tc, span * SUBLANES):
            rows = [pl.ds(t0 + j, span, stride=SUBLANES) for j in range(SUBLANES)]
            a_rows = [sa[c, idx, :] for idx in rows]
            b_rows = [sb[c, idx, :] for idx in rows]
            acc_a, acc_b = [a_rows[0]], [b_rows[0]]
            for j in range(1, SUBLANES):
                acc_b.append(a_rows[j] * acc_b[-1] + b_rows[j])
                acc_a.append(a_rows[j] * acc_a[-1])
            if chained:
                ta, tb = acc_a[-1], acc_b[-1]
                shift = 1
                while shift < span:
                    valid = tile_row >= shift
                    tb = jnp.where(valid, ta * pltpu.roll(tb, shift, axis=0) + tb, tb)
                    ta = jnp.where(valid, ta * pltpu.roll(ta, shift, axis=0), ta)
                    shift *= 2
                state_out = ta * carry + tb
                state_in = jnp.where(tile_row == 0, carry, pltpu.roll(state_out, 1, axis=0))
                acc_b = [pa * state_in + pb for pa, pb in zip(acc_a, acc_b)]
                carry = state_out[span - 1:span, :]
            for j, idx in enumerate(rows):
                sh[c, idx, :] = acc_b[j]
        if chained:
            carry_ref[c] = jnp.broadcast_to(carry, (SUBLANES, LANES))
    h = jnp.concatenate([sh[c] for c in range(slabs)], axis=-1)

    rec_ref[...] = (h * jax.nn.gelu(y_ref[...])).astype(rec_ref.dtype)
    if chained:
        @pl.when(step == pl.num_programs(2) - 1)
        def _():
            hout_ref[...] = h[tc - 1:tc, :]
    else:
        hout_ref[...] = h


def _rglru_params_specs(ch, idx):
    nblk = ch // LRU_BLOCK_W
    return [
        pl.BlockSpec((CONV_W, ch), lambda *g: (0, idx(*g))),
        pl.BlockSpec((1, ch), lambda *g: (0, idx(*g))),
        pl.BlockSpec((nblk, LRU_BLOCK_W, 2 * LRU_BLOCK_W), lambda *g: (idx(*g), 0, 0)),
        pl.BlockSpec((1, ch), lambda *g: (0, idx(*g))),
        pl.BlockSpec((1, ch), lambda *g: (0, idx(*g))),
        pl.BlockSpec((1, ch), lambda *g: (0, idx(*g))),
    ]


def _rglru_prompt(z3, x_col, y_col, params, *, d_rnn, tc, ch):
    batch, seq, _ = z3.shape
    nc = d_rnn // ch
    rec, hlast = pl.pallas_call(
        functools.partial(_rglru_body, chained=True),
        grid=(batch, nc, seq // tc),
        in_specs=[
            pl.BlockSpec((None, tc, ch), lambda b, c, t: (b, t, x_col // ch + c)),
            pl.BlockSpec((None, tc, ch), lambda b, c, t: (b, t, y_col // ch + c)),
        ] + _rglru_params_specs(ch, lambda b, c, t: c),
        out_specs=[
            pl.BlockSpec((None, tc, ch), lambda b, c, t: (b, t, c)),
            pl.BlockSpec((None, 1, ch), lambda b, c, t: (b, 0, c)),
        ],
        out_shape=[jax.ShapeDtypeStruct((batch, seq, d_rnn), BF16),
                   jax.ShapeDtypeStruct((batch, 1, d_rnn), F32)],
        scratch_shapes=[pltpu.VMEM((ch // LANES, tc, LANES), F32) for _ in range(3)]
        + [pltpu.VMEM((SUBLANES, ch), F32), pltpu.VMEM((ch // LANES, SUBLANES, LANES), F32)],
        compiler_params=_params("parallel", "parallel", "arbitrary"),
        name="rglru_prompt",
    )(z3, z3, *params)
    return rec, hlast


def _rglru_sample(x8, y8, h8, params, *, tc, ch):
    rows, d_rnn = x8.shape
    nc = d_rnn // ch
    row_spec = pl.BlockSpec((tc, ch), lambda c, t: (t, c))
    return pl.pallas_call(
        functools.partial(_rglru_body, chained=False),
        grid=(nc, rows // tc),
        in_specs=[row_spec, row_spec, row_spec] + _rglru_params_specs(ch, lambda c, t: c),
        out_specs=[row_spec, row_spec],
        out_shape=[jax.ShapeDtypeStruct((rows, d_rnn), F32),
                   jax.ShapeDtypeStruct((rows, d_rnn), F32)],
        scratch_shapes=[pltpu.VMEM((ch // LANES, tc, LANES), F32) for _ in range(3)],
        compiler_params=_params("parallel", "parallel"),
        name="rglru_sample",
    )(x8, y8, h8, *params)


def _merge_body(x_ref, attn_ref, rec_ref, ga0_ref, ga1_ref, gr0_ref, gr1_ref,
                wba_ref, wbr_ref, wout_ref, o_ref):
    pa = _dot(attn_ref[...].astype(BF16), wba_ref[...])
    pr = _dot(rec_ref[...].astype(BF16), wbr_ref[...])
    ga = jnp.concatenate([ga0_ref[...], ga1_ref[...]], axis=-1)
    gr = jnp.concatenate([gr0_ref[...], gr1_ref[...]], axis=-1)
    merged = jax.nn.sigmoid(ga) * pa + jax.nn.sigmoid(gr) * pr
    o_ref[...] = x_ref[...] + _dot(merged.astype(BF16), wout_ref[...])


def _merge(x, attn, rec, z, g_col, wba, wbr, wout, *, tm):
    m, d = x.shape
    half = d // 2
    gb = g_col // half

    def whole(w):
        return pl.BlockSpec(w.shape, lambda i: (0, 0), pipeline_mode=pl.Buffered(1))

    def gate_spec(k):
        return pl.BlockSpec((tm, half), lambda i, k=k: (i, gb + k))

    return pl.pallas_call(
        _merge_body,
        grid=(m // tm,),
        in_specs=[
            pl.BlockSpec((tm, d), lambda i: (i, 0)),
            pl.BlockSpec((tm, attn.shape[1]), lambda i: (i, 0)),
            pl.BlockSpec((tm, rec.shape[1]), lambda i: (i, 0)),
            gate_spec(0), gate_spec(1), gate_spec(2), gate_spec(3),
            whole(wba), whole(wbr), whole(wout),
        ],
        out_specs=pl.BlockSpec((tm, d), lambda i: (i, 0)),
        out_shape=jax.ShapeDtypeStruct((m, d), F32),
        compiler_params=_params("parallel"),
        name="merge",
    )(x, attn, rec, z, z, z, z, wba, wbr, wout)


def _ple_body(x_ref, pe_ref, lnp_ref, wpg_ref, wpp_ref, lnf_ref, y_ref):
    x = x_ref[...]
    hn = _rmsnorm(x, lnp_ref[...]).astype(BF16)
    gate = jax.nn.sigmoid(_dot(hn, wpg_ref[...]))
    proj = _dot(pe_ref[...].astype(BF16), wpp_ref[...])
    y_ref[...] = _rmsnorm(x + gate * proj, lnf_ref[...])


def _ple(x, pe, lnp, wpg, wpp, lnf, *, tm):
    m, d = x.shape

    def whole(w):
        return pl.BlockSpec(w.shape, lambda i: (0, 0), pipeline_mode=pl.Buffered(1))

    return pl.pallas_call(
        _ple_body,
        grid=(m // tm,),
        in_specs=[
            pl.BlockSpec((tm, d), lambda i: (i, 0)),
            pl.BlockSpec((tm, pe.shape[1]), lambda i: (i, 0)),
            pl.BlockSpec((1, d), lambda i: (0, 0)),
            whole(wpg), whole(wpp),
            pl.BlockSpec((1, d), lambda i: (0, 0)),
        ],
        out_specs=pl.BlockSpec((tm, d), lambda i: (i, 0)),
        out_shape=jax.ShapeDtypeStruct((m, d), F32),
        compiler_params=_params("parallel"),
        name="ple",
    )(x, pe, lnp, wpg, wpp, lnf)


def kernel(x_prompt, x_sample, cache_kv_w128, cache_kv_w512, cache_kv_w2048, state_conv, state_rglru, p_prompt, p_sample, ln_ffn1, w_ffn1_gate, w_ffn1_up, w_ffn1_down, ln_mix, w_in, conv_w, conv_b, w_rgate, b_rgate, w_igate, b_igate, lru_lambda, w_branch_attn, w_branch_rec, w_out, ln_ffn2, w_ffn2_gate, w_ffn2_up, w_ffn2_down, ln_ple, w_ple_gate, w_ple_proj, ln_final):
    batch, seq, d = x_prompt.shape
    nb, nq, _ = x_sample.shape
    past_len = cache_kv_w2048.shape[4]
    d_rnn = conv_w.shape[-1]
    depth = w_in.shape[0]
    assert depth == 1

    xp = x_prompt.reshape(batch * seq, d)
    xs = x_sample.reshape(nb * nq, d)
    cos_p, sin_p = _rope_tables(jnp.arange(seq))
    cos_s, sin_s = _rope_tables(past_len + (jnp.arange(nb * nq) % nq))

    l = 0
    row = lambda v: v[l].reshape(1, -1)
    wri = jnp.concatenate([w_rgate[l], w_igate[l]], axis=-1).astype(BF16)
    rg_params = (conv_w[l], row(conv_b), wri, row(b_rgate), row(b_igate), row(lru_lambda))
    w1 = (w_ffn1_gate, w_ffn1_up, w_ffn1_down)
    w2 = (w_ffn2_gate, w_ffn2_up, w_ffn2_down)
    ffn_tiles = dict(tm=1024, tf=256)
    win = w_in[l].astype(BF16)
    wba = w_branch_attn[l].astype(BF16)
    wbr = w_branch_rec[l].astype(BF16)
    wout = w_out[l].astype(BF16)
    wpg = w_ple_gate[l].astype(BF16)
    wpp = w_ple_proj[l].astype(BF16)
    lnf = ln_final.reshape(1, -1)

    x_col = 3 * QKV_WIDTH
    y_col = x_col + d_rnn
    g_col = y_col + d_rnn
    tm = 512

    xp = _ffn(xp, row(ln_ffn1), *w1, **ffn_tiles)
    zp = _inproj(xp, row(ln_mix), win, cos_p, sin_p, tm=1024, tn=1024)
    zp3 = zp.reshape(batch, seq, -1)
    attn_p, kv1_p, kv2_p, kv3_p = _attn_prompt(zp3, batch, seq)
    rec_p, hlast_p = _rglru_prompt(zp3, x_col, y_col, rg_params, d_rnn=d_rnn, tc=256, ch=1024)
    conv_p = zp3[:, seq - (CONV_W - 1):, x_col:x_col + d_rnn][None]
    xp = _merge(xp, attn_p.reshape(batch * seq, -1), rec_p.reshape(batch * seq, -1), zp, g_col,
                wba, wbr, wout, tm=256)
    xp = _ffn(xp, row(ln_ffn2), *w2, **ffn_tiles)
    yp = _ple(xp, p_prompt[l].reshape(batch * seq, -1), row(ln_ple), wpg, wpp, lnf, tm=256)

    xs = _ffn(xs, row(ln_ffn1), *w1, **ffn_tiles)
    zs = _inproj(xs, row(ln_mix), win, cos_s, sin_s, tm=tm, tn=512)
    qkv = zs[:, :3 * QKV_WIDTH].reshape(nb, nq, 3, N_GROUPS, HEADS_PER_GROUP, HEAD_DIM)
    qkv = qkv.transpose(2, 0, 3, 4, 1, 5)
    attn_s = _attn_sample(qkv[0], qkv[1], qkv[2], cache_kv_w128, cache_kv_w512, cache_kv_w2048)
    attn_s = attn_s.transpose(0, 2, 1, 3).reshape(nb * nq, -1)
    kv_s = [jnp.stack([qkv[1][:, g], qkv[2][:, g]], axis=1)[None] for g in range(N_GROUPS)]

    x_rec = zs[:, x_col:x_col + d_rnn].reshape(nb, nq, d_rnn)
    y_rec = zs[:, y_col:y_col + d_rnn].reshape(nb, nq, d_rnn)
    zero1 = jnp.zeros((nb, 1, d_rnn), F32)
    x8 = jnp.concatenate([zero1, state_conv[l], x_rec], axis=1).reshape(nb * SUBLANES, d_rnn)
    y8 = jnp.concatenate([jnp.zeros((nb, CONV_W, d_rnn), F32), y_rec], axis=1).reshape(nb * SUBLANES, d_rnn)
    h8 = jnp.concatenate([jnp.zeros((nb, CONV_W - 1, d_rnn), F32), state_rglru[l][:, None],
                          jnp.zeros((nb, nq, d_rnn), F32)], axis=1).reshape(nb * SUBLANES, d_rnn)
    rec8, hs8 = _rglru_sample(x8, y8, h8, rg_params, tc=256, ch=1024)
    rec_s = rec8.reshape(nb, SUBLANES, d_rnn)[:, CONV_W:].reshape(nb * nq, d_rnn)
    hlast_s = hs8.reshape(nb, SUBLANES, d_rnn)[:, SUBLANES - 1][None]
    conv_s = jnp.concatenate([state_conv[l], x_rec], axis=1)[:, -(CONV_W - 1):][None]
    xs = _merge(xs, attn_s, rec_s, zs, g_col, wba, wbr, wout, tm=256)
    xs = _ffn(xs, row(ln_ffn2), *w2, **ffn_tiles)
    ys = _ple(xs, p_sample[l].reshape(nb * nq, -1), row(ln_ple), wpg, wpp, lnf, tm=256)

    return (yp.reshape(batch, seq, d), ys.reshape(nb, nq, d),
            kv1_p, kv2_p, kv3_p, conv_p, hlast_p.reshape(1, batch, d_rnn),
            kv_s[0], kv_s[1], kv_s[2], conv_s, hlast_s)
```

```python
import functools
import math

import jax
import jax.numpy as jnp
from jax import lax
from jax.experimental import pallas as pl
from jax.experimental.pallas import tpu as pltpu

F32 = jnp.float32
BF16 = jnp.bfloat16

HEAD_DIM = 128
HEADS_PER_GROUP = 8
ATTN_GROUPS = ((128, 1), (512, 4), (2048, 16))
N_GROUPS = 3
QKV_WIDTH = N_GROUPS * HEADS_PER_GROUP * HEAD_DIM
ATTN_SCALE = HEAD_DIM ** -0.5
BAND_BLOCK = 128
ROPE_THETA = 10000.0
LRU_BLOCK_W = 128
LRU_C = 8.0
CONV_W = 4
EPS = 1e-6

SUBLANES = 8
LANES = 128
VMEM_LIMIT = 48 * 1024 * 1024
FFN_VMEM_LIMIT = 56 * 1024 * 1024
NEG_INF = float("-inf")
PROMPT_BLOCKS_PER_TRIP = 16
SAMPLE_HEADS_PER_TRIP = 4
ROPE_ROW_BAND = 64
ROW_BAND = 256
SCAN_TILES_PER_PASS = 16


def _params(*semantics):
    return pltpu.CompilerParams(dimension_semantics=semantics, vmem_limit_bytes=VMEM_LIMIT)


def _rmsnorm(x, g):
    return x * lax.rsqrt(jnp.mean(x * x, axis=-1, keepdims=True) + EPS) * g


def _dot(a, b):
    return jnp.dot(a, b, preferred_element_type=F32)


def _dot_nt(a, b):
    return lax.dot_general(a, b, (((1,), (1,)), ((), ())), preferred_element_type=F32)


def _ffn_body(x_ref, ln_ref, wg_ref, wu_ref, wd_ref, o_ref, h_ref):
    f = pl.program_id(1)

    def half_swiglu(h):
        g = _dot(h, wg_ref[...].astype(BF16))
        u = _dot(h, wu_ref[...].astype(BF16))
        a = (0.5 * g * jax.nn.sigmoid(g)) * u
        return _dot(a.astype(BF16), wd_ref[...].astype(BF16))

    @pl.when(f == 0)
    def _():
        for r0 in range(0, x_ref.shape[0], ROW_BAND):
            rows = slice(r0, r0 + ROW_BAND)
            x = x_ref[rows, :]
            h = _rmsnorm(x, ln_ref[...]).astype(BF16)
            h_ref[rows, :] = h
            o_ref[rows, :] = x + half_swiglu(h)

    @pl.when(f > 0)
    def _():
        o_ref[...] += half_swiglu(h_ref[...])


def _ffn(x, ln, wg, wu, wd, *, tm, tf):
    m, d = x.shape
    tm = min(tm, m)
    ff = wg.shape[2]
    return pl.pallas_call(
        _ffn_body,
        grid=(m // tm, ff // tf),
        in_specs=[
            pl.BlockSpec((tm, d), lambda i, f: (i, 0)),
            pl.BlockSpec((1, d), lambda i, f: (0, 0)),
            pl.BlockSpec((None, d, tf), lambda i, f: (0, 0, f)),
            pl.BlockSpec((None, d, tf), lambda i, f: (0, 0, f)),
            pl.BlockSpec((None, tf, d), lambda i, f: (0, f, 0)),
        ],
        out_specs=pl.BlockSpec((tm, d), lambda i, f: (i, 0)),
        out_shape=jax.ShapeDtypeStruct((m, d), F32),
        scratch_shapes=[pltpu.VMEM((tm, d), BF16)],
        compiler_params=pltpu.CompilerParams(dimension_semantics=("parallel", "arbitrary"),
                                             vmem_limit_bytes=FFN_VMEM_LIMIT),
        name="ffn",
    )(x, ln, wg, wu, wd)


def _inproj_body(x_ref, ln_ref, w_ref, cos_ref, sin_ref, z_ref, h_ref, zb0, zb1, *,
                 rope_tiles, col_tiles):
    j = pl.program_id(1)
    bufs = (zb0, zb1)
    assert 0 < rope_tiles < col_tiles

    def finish_rope(src):
        for r0 in range(0, z_ref.shape[0], ROPE_ROW_BAND):
            rows = slice(r0, r0 + ROPE_ROW_BAND)
            cos = cos_ref[rows, :]
            sin = sin_ref[rows, :]
            for c in range(z_ref.shape[1] // HEAD_DIM):
                cols = slice(c * HEAD_DIM, (c + 1) * HEAD_DIM)
                zc = src[rows, cols]
                z_ref[rows, cols] = zc * cos + pltpu.roll(zc, HEAD_DIM // 2, axis=1) * sin

    @pl.when(j == 0)
    def _():
        h_ref[...] = _rmsnorm(x_ref[...], ln_ref[...]).astype(BF16)
        zb0[...] = _dot(h_ref[...], w_ref[...])

    for parity in (0, 1):
        cur, prev = bufs[parity], bufs[1 - parity]
        mine = (j % 2) == parity

        @pl.when(mine & (j >= 1) & (j < rope_tiles))
        def _(cur=cur, prev=prev):
            cur[...] = _dot(h_ref[...], w_ref[...])
            finish_rope(prev)

    @pl.when(j == rope_tiles)
    def _():
        finish_rope(bufs[(rope_tiles - 1) % 2])

    @pl.when(j > rope_tiles)
    def _():
        z_ref[...] = _dot(h_ref[...], w_ref[...])


def _inproj(x, ln, w, cos, sin, *, tm, tn):
    m, d = x.shape
    tm = min(tm, m)
    n = w.shape[1]
    col_tiles = n // tn
    rope_tiles = 2 * QKV_WIDTH // tn
    pos_tiles = cos.shape[0] // tm
    return pl.pallas_call(
        functools.partial(_inproj_body, rope_tiles=rope_tiles, col_tiles=col_tiles),
        grid=(m // tm, col_tiles + 1),
        in_specs=[
            pl.BlockSpec((tm, d), lambda i, j: (i, 0)),
            pl.BlockSpec((1, d), lambda i, j: (0, 0)),
            pl.BlockSpec((d, tn), lambda i, j: (0, jnp.where(j < rope_tiles, j, j - 1))),
            pl.BlockSpec((tm, HEAD_DIM), lambda i, j: (i % pos_tiles, 0)),
            pl.BlockSpec((tm, HEAD_DIM), lambda i, j: (i % pos_tiles, 0)),
        ],
        out_specs=pl.BlockSpec((tm, tn), lambda i, j: (i, jnp.maximum(j - 1, 0))),
        out_shape=jax.ShapeDtypeStruct((m, n), F32),
        scratch_shapes=[pltpu.VMEM((tm, d), BF16), pltpu.VMEM((tm, tn), F32),
                        pltpu.VMEM((tm, tn), F32)],
        compiler_params=_params("parallel", "arbitrary"),
        name="inproj",
    )(x, ln, w, cos, sin)


def _rope_tables(pos):
    half = HEAD_DIM // 2
    inv_freq = jnp.exp(-math.log(ROPE_THETA) * jnp.arange(half, dtype=F32) / half)
    ang = pos.astype(F32)[:, None] * inv_freq[None, :]
    cos, sin = jnp.cos(ang), jnp.sin(ang)
    return jnp.concatenate([cos, cos], axis=-1), jnp.concatenate([-sin, sin], axis=-1)


def _attn_prompt_body(q1, q2, q3, k1, k2, k3, v1, v2, v3,
                      attn_ref, kv128_ref, kv512_ref, kv2048_ref,
                      o_s, m_s, l_s):
    seq = q1.shape[0]
    blk = BAND_BLOCK
    for kv_ref, k_ref, v_ref in ((kv128_ref, k1, v1), (kv512_ref, k2, v2), (kv2048_ref, k3, v3)):
        keep = kv_ref.shape[1]
        kv_ref[0] = k_ref[pl.ds(seq - keep, keep), :]
        kv_ref[1] = v_ref[pl.ds(seq - keep, keep), :]

    row = lax.broadcasted_iota(jnp.int32, (blk, blk), 0)
    col = lax.broadcasted_iota(jnp.int32, (blk, blk), 1)
    own_mask = col <= row
    prev_mask = col >= row
    ones = jnp.ones((blk, HEAD_DIM), BF16)

    groups = ((q1, k1, v1), (q2, k2, v2), (q3, k3, v3))
    for g, ((q_ref, k_ref, v_ref), (_, dil)) in enumerate(zip(groups, ATTN_GROUPS)):
        nblk = seq // dil // blk
        has_prev = nblk > 1

        def rows(ref, start, dil=dil):
            if dil == 1:
                return ref[pl.ds(start, blk), :]
            return ref[pl.ds(start, blk, stride=dil), :]

        def sub(it, carry, g=g, dil=dil, nblk=nblk, has_prev=has_prev,
                q_ref=q_ref, k_ref=k_ref, v_ref=v_ref, rows=rows):
            work = []
            for u in range(PROMPT_BLOCKS_PER_TRIP):
                p = it * PROMPT_BLOCKS_PER_TRIP + u
                w = {"r": p // nblk, "n": p % nblk}
                w["qs"] = w["r"] + dil * (w["n"] * blk)
                w["q"] = rows(q_ref, w["qs"]).astype(BF16)
                w["kc"] = rows(k_ref, w["qs"]).astype(BF16)
                if has_prev:
                    w["ps"] = w["r"] + dil * (jnp.maximum(w["n"] - 1, 0) * blk)
                    w["kp"] = rows(k_ref, w["ps"]).astype(BF16)
                work.append(w)
            for w in work:
                w["s_own"] = jnp.where(own_mask, _dot_nt(w["q"], w["kc"]) * ATTN_SCALE, NEG_INF)
                if has_prev:
                    pm = jnp.logical_and(prev_mask, w["n"] > 0)
                    w["s_prev"] = jnp.where(pm, _dot_nt(w["q"], w["kp"]) * ATTN_SCALE, NEG_INF)
            for w in work:
                m = jnp.max(w["s_own"], axis=-1, keepdims=True)
                if has_prev:
                    m = jnp.maximum(m, jnp.max(w["s_prev"], axis=-1, keepdims=True))
                w["m"] = m
                w["e_own"] = jnp.exp(w["s_own"] - m).astype(BF16)
                if has_prev:
                    w["e_prev"] = jnp.exp(w["s_prev"] - m).astype(BF16)
            for w in work:
                acc = _dot(w["e_own"], rows(v_ref, w["qs"]).astype(BF16))
                l = _dot(w["e_own"], ones)
                if has_prev:
                    acc = acc + _dot(w["e_prev"], rows(v_ref, w["ps"]).astype(BF16))
                    l = l + _dot(w["e_prev"], ones)
                w["acc"] = acc
                w["l"] = l
            for w in work:
                if dil == 1:
                    idx = pl.ds(w["qs"], blk)
                else:
                    idx = pl.ds(w["qs"], blk, stride=dil)
                o_s[g, idx, :] = w["acc"]
                m_s[g, idx, :] = jnp.broadcast_to(w["m"], (blk, HEAD_DIM))
                l_s[g, idx, :] = w["l"]
            return carry

        lax.fori_loop(0, dil * nblk // PROMPT_BLOCKS_PER_TRIP, sub, 0)

    def combine(t, carry):
        idx = pl.ds(pl.multiple_of(t * blk, blk), blk)
        ms = [m_s[g, idx, :] for g in range(N_GROUPS)]
        mx = jnp.maximum(jnp.maximum(ms[0], ms[1]), ms[2])
        num = jnp.zeros((blk, HEAD_DIM), F32)
        den = jnp.zeros((blk, HEAD_DIM), F32)
        for g in range(N_GROUPS):
            w = jnp.exp(ms[g] - mx)
            num = num + w * o_s[g, idx, :]
            den = den + w * l_s[g, idx, :]
        attn_ref[idx, :] = (num / den).astype(attn_ref.dtype)
        return carry

    lax.fori_loop(0, seq // blk, combine, 0)


def _attn_prompt(z3, batch, seq):
    hd = HEAD_DIM
    nh = HEADS_PER_GROUP

    def col_spec(base):
        return pl.BlockSpec((None, seq, hd), lambda b, h, base=base: (b, 0, base + h))

    in_specs = [col_spec((t * N_GROUPS + g) * nh) for t in range(3) for g in range(N_GROUPS)]
    keeps = [min(w, seq) for w, _ in ATTN_GROUPS]
    kv_shapes = [jax.ShapeDtypeStruct((1, batch, 2, nh, keep, hd), F32) for keep in keeps]
    kv_specs = [pl.BlockSpec((None, None, 2, None, keep, hd), lambda b, h: (0, b, 0, h, 0, 0))
                for keep in keeps]
    return pl.pallas_call(
        _attn_prompt_body,
        grid=(batch, nh),
        in_specs=in_specs,
        out_specs=[pl.BlockSpec((None, seq, hd), lambda b, h: (b, 0, h))] + kv_specs,
        out_shape=[jax.ShapeDtypeStruct((batch, seq, nh * hd), BF16)] + kv_shapes,
        scratch_shapes=[pltpu.VMEM((N_GROUPS, seq, hd), F32) for _ in range(3)],
        compiler_params=_params("parallel", "parallel"),
        name="attn_prompt",
    )(*([z3] * 9))


def _attn_sample_body(q_ref, k_ref, v_ref, c1_ref, c2_ref, c3_hbm, attn_ref, c3_buf, c3_sem):
    nq = q_ref.shape[2]
    hd = HEAD_DIM
    pad = SUBLANES - nq

    b = pl.program_id(0)
    slot = b % 2

    def c3_copy(seq, s):
        return pltpu.make_async_copy(c3_hbm.at[0, seq, :, :, :, pl.ds(0, nq), :],
                                     c3_buf.at[s, :, :, :, pl.ds(0, nq), :], c3_sem.at[s])

    @pl.when(b == 0)
    def _():
        c3_buf[:, :, :, :, pl.ds(nq, pad), :] = jnp.zeros(c3_buf.shape[:4] + (pad, hd), F32)
        c3_copy(0, 0).start()

    @pl.when(b + 1 < pl.num_programs(0))
    def _():
        c3_copy(b + 1, 1 - slot).start()

    c3_copy(b, slot).wait()
    caches = (c1_ref, c2_ref, c3_buf.at[slot])

    def cache_rows(g, kv, h):
        x = caches[g][kv, h]
        return x.reshape(-1, hd).astype(BF16)

    def mask_for(g):
        n = caches[g].shape[2] * (caches[g].shape[3] if g == 2 else 1)
        row = lax.broadcasted_iota(jnp.int32, (SUBLANES, n), 0)
        col = lax.broadcasted_iota(jnp.int32, (SUBLANES, n), 1)
        if g == 0:
            return col >= row
        step = ATTN_GROUPS[1][1] if g == 1 else caches[g].shape[3]
        assert step & (step - 1) == 0
        return jnp.bitwise_and(col, step - 1) == row

    masks = [mask_for(g) for g in range(N_GROUPS)]
    qrow = lax.broadcasted_iota(jnp.int32, (SUBLANES, 1), 0)

    def scores(h):
        parts = []
        news = []
        for g in range(N_GROUPS):
            q = jnp.concatenate([q_ref[g, h], jnp.zeros((pad, hd), F32)], axis=0).astype(BF16)
            s = _dot_nt(q, cache_rows(g, 0, h)) * ATTN_SCALE
            parts.append((jnp.where(masks[g], s, NEG_INF), g))
            qf = q.astype(F32)
            kn = k_ref[g, h].astype(BF16).astype(F32)
            vn = v_ref[g, h].astype(BF16).astype(F32)
            if g == 0:
                for c in range(nq):
                    sn = jnp.sum(qf * kn[c:c + 1, :], axis=-1, keepdims=True) * ATTN_SCALE
                    news.append((jnp.where(qrow >= c, sn, NEG_INF), vn[c:c + 1, :]))
            else:
                knp = jnp.concatenate([kn, jnp.zeros((pad, hd), F32)], axis=0)
                sn = jnp.sum(qf * knp, axis=-1, keepdims=True) * ATTN_SCALE
                for c in range(nq):
                    news.append((jnp.where(qrow == c, sn, NEG_INF), vn[c:c + 1, :]))
        return parts, news

    def softmax(parts, news):
        m = parts[0][0].max(axis=-1, keepdims=True)
        for s, _ in parts[1:]:
            m = jnp.maximum(m, s.max(axis=-1, keepdims=True))
        for s, _ in news:
            m = jnp.maximum(m, s)
        es = [jnp.exp(s - m) for s, _ in parts]
        l = jnp.zeros((SUBLANES, 1), F32)
        for e in es:
            l = l + jnp.sum(e, axis=-1, keepdims=True)
        acc = jnp.zeros((SUBLANES, hd), F32)
        for s, v in news:
            e = jnp.exp(s - m)
            l = l + e
            acc = acc + e.astype(BF16).astype(F32) * v
        return es, l, acc

    def trip(it, carry):
        heads = [it * SAMPLE_HEADS_PER_TRIP + u for u in range(SAMPLE_HEADS_PER_TRIP)]
        staged = [scores(h) for h in heads]
        soft = [softmax(parts, news) for parts, news in staged]
        outs = []
        for h, (parts, _), (es, l, acc) in zip(heads, staged, soft):
            for e, (_, g) in zip(es, parts):
                acc = acc + _dot(e.astype(BF16), cache_rows(g, 1, h))
            outs.append((acc / l)[:nq, :])
        for h, out in zip(heads, outs):
            attn_ref[h] = out
        return carry

    lax.fori_loop(0, HEADS_PER_GROUP // SAMPLE_HEADS_PER_TRIP, trip, 0)


def _attn_sample(q_s, k_s, v_s, c128, c512, c2048):
    nb, _, nh, nq, hd = q_s.shape
    new_spec = pl.BlockSpec((None, N_GROUPS, nh, nq, hd), lambda b: (b, 0, 0, 0, 0))
    for cache, (window, _) in zip((c128, c512, c2048), ATTN_GROUPS):
        assert cache.shape[4] == window, "the caches must hold a full window"
    assert nq <= ATTN_GROUPS[1][1]
    dil3 = ATTN_GROUPS[2][1]
    rows3 = c2048.shape[4] // dil3
    c2048 = c2048.reshape(1, nb, 2, nh, rows3, dil3, hd)
    return pl.pallas_call(
        _attn_sample_body,
        grid=(nb,),
        in_specs=[
            new_spec, new_spec, new_spec,
            pl.BlockSpec((None, None, 2, nh, c128.shape[4], hd), lambda b: (0, b, 0, 0, 0, 0)),
            pl.BlockSpec((None, None, 2, nh, c512.shape[4], hd), lambda b: (0, b, 0, 0, 0, 0)),
            pl.BlockSpec(memory_space=pl.ANY),
        ],
        out_specs=pl.BlockSpec((None, nh, nq, hd), lambda b: (b, 0, 0, 0)),
        out_shape=jax.ShapeDtypeStruct((nb, nh, nq, hd), F32),
        scratch_shapes=[pltpu.VMEM((2, 2, nh, rows3, SUBLANES, hd), F32),
                        pltpu.SemaphoreType.DMA((2,))],
        compiler_params=_params("arbitrary"),
        name="attn_sample",
    )(q_s, k_s, v_s, c128, c512, c2048)


def _rglru_body(*refs, chained):
    if chained:
        (x_ref, y_ref, cw_ref, cb_ref, wri_ref, br_ref, bi_ref, lam_ref,
         rec_ref, hout_ref, sa, sb, sh, tail_ref, carry_ref) = refs
    else:
        (x_ref, y_ref, h0_ref, cw_ref, cb_ref, wri_ref, br_ref, bi_ref, lam_ref,
         rec_ref, hout_ref, sa, sb, sh) = refs
    tc, ch = x_ref.shape
    tiles = (tc // SUBLANES, SUBLANES, ch)
    step = pl.program_id(2) if chained else None
    row8 = lax.broadcasted_iota(jnp.int32, tiles, 1)

    cur = x_ref[...]
    if chained:
        @pl.when(step == 0)
        def _():
            tail_ref[...] = jnp.zeros_like(tail_ref)
            carry_ref[...] = jnp.zeros_like(carry_ref)

        prev = jnp.concatenate([tail_ref[...], cur[:tc - SUBLANES, :]], axis=0)
        tail_ref[...] = cur[tc - SUBLANES:, :]
    else:
        prev = cur
    cur = cur.reshape(tiles)
    prev = prev.reshape(tiles)

    xc = cb_ref[...]
    for j in range(CONV_W):
        d = CONV_W - 1 - j
        xs = cur if d == 0 else pltpu.roll(jnp.where(row8 >= SUBLANES - d, prev, cur), d, axis=1)
        xc = xc + xs * cw_ref[j:j + 1, :]

    xc2 = xc.reshape(tc, ch)
    pre_r, pre_i = [], []
    for n in range(ch // LRU_BLOCK_W):
        blk = xc2[:, n * LRU_BLOCK_W:(n + 1) * LRU_BLOCK_W].astype(BF16)
        ri = _dot(blk, wri_ref[n])
        pre_r.append(ri[:, :LRU_BLOCK_W])
        pre_i.append(ri[:, LRU_BLOCK_W:])
    r = jax.nn.sigmoid(jnp.concatenate(pre_r, axis=-1) + br_ref[...])
    i = jax.nn.sigmoid(jnp.concatenate(pre_i, axis=-1) + bi_ref[...])
    log_a = r * (-LRU_C * jax.nn.softplus(-lam_ref[...]))
    a = jnp.exp(log_a)
    th = jnp.tanh(log_a)
    p = -2.0 * th
    w = 1.0 - th
    mult = jnp.where(p > 0.0, p * lax.rsqrt(p * w), 0.0)
    b = mult * (i * xc2)

    if not chained:
        a = jnp.where(row8 < CONV_W, 1.0, a.reshape(tiles)).reshape(tc, ch)
        b = jnp.where(row8 < CONV_W, h0_ref[...].reshape(tiles), b.reshape(tiles)).reshape(tc, ch)

    slabs = ch // LANES
    for c in range(slabs):
        sa[c] = a[:, c * LANES:(c + 1) * LANES]
        sb[c] = b[:, c * LANES:(c + 1) * LANES]
    span = SCAN_TILES_PER_PASS
    tile_row = lax.broadcasted_iota(jnp.int32, (span, LANES), 0)
    for c in range(slabs):
        carry = carry_ref[c, 0:1, :] if chained else None
        for t0 in range(0, tc, span * SUBLANES):
            rows = [pl.ds(t0 + j, span, stride=SUBLANES) for j in range(SUBLANES)]
            a_rows = [sa[c, idx, :] for idx in rows]
            b_rows = [sb[c, idx, :] for idx in rows]
            acc_a, acc_b = [a_rows[0]], [b_rows[0]]
            for j in range(1, SUBLANES):
                acc_b.append(a_rows[j] * acc_b[-1] + b_rows[j])
                acc_a.append(a_rows[j] * acc_a[-1])
            if chained:
                ta, tb = acc_a[-1], acc_b[-1]
                shift = 1
                while shift < span:
                    valid = tile_row >= shift
                    tb = jnp.where(valid, ta * pltpu.roll(tb, shift, axis=0) + tb, tb)
                    ta = jnp.where(valid, ta * pltpu.roll(ta, shift, axis=0), ta)
                    shift *= 2
                state_out = ta * carry + tb
                state_in = jnp.where(tile_row == 0, carry, pltpu.roll(state_out, 1, axis=0))
                acc_b = [pa * state_in + pb for pa, pb in zip(acc_a, acc_b)]
                carry = state_out[span - 1:span, :]
            for j, idx in enumerate(rows):
                sh[c, idx, :] = acc_b[j]
        if chained:
            carry_ref[c] = jnp.broadcast_to(carry, (SUBLANES, LANES))
    h = jnp.concatenate([sh[c] for c in range(slabs)], axis=-1)

    rec_ref[...] = (h * jax.nn.gelu(y_ref[...])).astype(rec_ref.dtype)
    if chained:
        @pl.when(step == pl.num_programs(2) - 1)
        def _():
            hout_ref[...] = h[tc - 1:tc, :]
    else:
        hout_ref[...] = h


def _rglru_params_specs(ch, idx):
    nblk = ch // LRU_BLOCK_W
    return [
        pl.BlockSpec((CONV_W, ch), lambda *g: (0, idx(*g))),
        pl.BlockSpec((1, ch), lambda *g: (0, idx(*g))),
        pl.BlockSpec((nblk, LRU_BLOCK_W, 2 * LRU_BLOCK_W), lambda *g: (idx(*g), 0, 0)),
        pl.BlockSpec((1, ch), lambda *g: (0, idx(*g))),
        pl.BlockSpec((1, ch), lambda *g: (0, idx(*g))),
        pl.BlockSpec((1, ch), lambda *g: (0, idx(*g))),
    ]


def _rglru_prompt(z3, x_col, y_col, params, *, d_rnn, tc, ch):
    batch, seq, _ = z3.shape
    nc = d_rnn // ch
    rec, hlast = pl.pallas_call(
        functools.partial(_rglru_body, chained=True),
        grid=(batch, nc, seq // tc),
        in_specs=[
            pl.BlockSpec((None, tc, ch), lambda b, c, t: (b, t, x_col // ch + c)),
            pl.BlockSpec((None, tc, ch), lambda b, c, t: (b, t, y_col // ch + c)),
        ] + _rglru_params_specs(ch, lambda b, c, t: c),
        out_specs=[
            pl.BlockSpec((None, tc, ch), lambda b, c, t: (b, t, c)),
            pl.BlockSpec((None, 1, ch), lambda b, c, t: (b, 0, c)),
        ],
        out_shape=[jax.ShapeDtypeStruct((batch, seq, d_rnn), BF16),
                   jax.ShapeDtypeStruct((batch, 1, d_rnn), F32)],
        scratch_shapes=[pltpu.VMEM((ch // LANES, tc, LANES), F32) for _ in range(3)]
        + [pltpu.VMEM((SUBLANES, ch), F32), pltpu.VMEM((ch // LANES, SUBLANES, LANES), F32)],
        compiler_params=_params("parallel", "parallel", "arbitrary"),
        name="rglru_prompt",
    )(z3, z3, *params)
    return rec, hlast


def _rglru_sample(x8, y8, h8, params, *, tc, ch):
    rows, d_rnn = x8.shape
    nc = d_rnn // ch
    row_spec = pl.BlockSpec((tc, ch), lambda c, t: (t, c))
    return pl.pallas_call(
        functools.partial(_rglru_body, chained=False),
        grid=(nc, rows // tc),
        in_specs=[row_spec, row_spec, row_spec] + _rglru_params_specs(ch, lambda c, t: c),
        out_specs=[row_spec, row_spec],
        out_shape=[jax.ShapeDtypeStruct((rows, d_rnn), F32),
                   jax.ShapeDtypeStruct((rows, d_rnn), F32)],
        scratch_shapes=[pltpu.VMEM((ch // LANES, tc, LANES), F32) for _ in range(3)],
        compiler_params=_params("parallel", "parallel"),
        name="rglru_sample",
    )(x8, y8, h8, *params)


def _merge_body(x_ref, attn_ref, rec_ref, ga0_ref, ga1_ref, gr0_ref, gr1_ref,
                wba_ref, wbr_ref, wout_ref, o_ref):
    pa = _dot(attn_ref[...].astype(BF16), wba_ref[...])
    pr = _dot(rec_ref[...].astype(BF16), wbr_ref[...])
    ga = jnp.concatenate([ga0_ref[...], ga1_ref[...]], axis=-1)
    gr = jnp.concatenate([gr0_ref[...], gr1_ref[...]], axis=-1)
    merged = jax.nn.sigmoid(ga) * pa + jax.nn.sigmoid(gr) * pr
    o_ref[...] = x_ref[...] + _dot(merged.astype(BF16), wout_ref[...])


def _merge(x, attn, rec, z, g_col, wba, wbr, wout, *, tm):
    m, d = x.shape
    half = d // 2
    gb = g_col // half

    def whole(w):
        return pl.BlockSpec(w.shape, lambda i: (0, 0), pipeline_mode=pl.Buffered(1))

    def gate_spec(k):
        return pl.BlockSpec((tm, half), lambda i, k=k: (i, gb + k))

    return pl.pallas_call(
        _merge_body,
        grid=(m // tm,),
        in_specs=[
            pl.BlockSpec((tm, d), lambda i: (i, 0)),
            pl.BlockSpec((tm, attn.shape[1]), lambda i: (i, 0)),
            pl.BlockSpec((tm, rec.shape[1]), lambda i: (i, 0)),
            gate_spec(0), gate_spec(1), gate_spec(2), gate_spec(3),
            whole(wba), whole(wbr), whole(wout),
        ],
        out_specs=pl.BlockSpec((tm, d), lambda i: (i, 0)),
        out_shape=jax.ShapeDtypeStruct((m, d), F32),
        compiler_params=_params("parallel"),
        name="merge",
    )(x, attn, rec, z, z, z, z, wba, wbr, wout)


def _ple_body(x_ref, pe_ref, lnp_ref, wpg_ref, wpp_ref, lnf_ref, y_ref):
    for r0 in range(0, x_ref.shape[0], ROW_BAND):
        rows = slice(r0, r0 + ROW_BAND)
        x = x_ref[rows, :]
        hn = _rmsnorm(x, lnp_ref[...]).astype(BF16)
        gate = jax.nn.sigmoid(_dot(hn, wpg_ref[...]))
        proj = _dot(pe_ref[rows, :].astype(BF16), wpp_ref[...])
        y_ref[rows, :] = _rmsnorm(x + gate * proj, lnf_ref[...])


def _ple(x, pe, lnp, wpg, wpp, lnf, *, tm):
    m, d = x.shape

    def whole(w):
        return pl.BlockSpec(w.shape, lambda i: (0, 0), pipeline_mode=pl.Buffered(1))

    return pl.pallas_call(
        _ple_body,
        grid=(m // tm,),
        in_specs=[
            pl.BlockSpec((tm, d), lambda i: (i, 0)),
            pl.BlockSpec((tm, pe.shape[1]), lambda i: (i, 0)),
            pl.BlockSpec((1, d), lambda i: (0, 0)),
            whole(wpg), whole(wpp),
            pl.BlockSpec((1, d), lambda i: (0, 0)),
        ],
        out_specs=pl.BlockSpec((tm, d), lambda i: (i, 0)),
        out_shape=jax.ShapeDtypeStruct((m, d), F32),
        compiler_params=_params("parallel"),
        name="ple",
    )(x, pe, lnp, wpg, wpp, lnf)


def kernel(x_prompt, x_sample, cache_kv_w128, cache_kv_w512, cache_kv_w2048, state_conv, state_rglru, p_prompt, p_sample, ln_ffn1, w_ffn1_gate, w_ffn1_up, w_ffn1_down, ln_mix, w_in, conv_w, conv_b, w_rgate, b_rgate, w_igate, b_igate, lru_lambda, w_branch_attn, w_branch_rec, w_out, ln_ffn2, w_ffn2_gate, w_ffn2_up, w_ffn2_down, ln_ple, w_ple_gate, w_ple_proj, ln_final):
    batch, seq, d = x_prompt.shape
    nb, nq, _ = x_sample.shape
    past_len = cache_kv_w2048.shape[4]
    d_rnn = conv_w.shape[-1]
    depth = w_in.shape[0]
    assert depth == 1

    xp = x_prompt.reshape(batch * seq, d)
    xs = x_sample.reshape(nb * nq, d)
    cos_p, sin_p = _rope_tables(jnp.arange(seq))
    cos_s, sin_s = _rope_tables(past_len + (jnp.arange(nb * nq) % nq))

    l = 0
    row = lambda v: v[l].reshape(1, -1)
    wri = jnp.concatenate([w_rgate[l], w_igate[l]], axis=-1).astype(BF16)
    rg_params = (conv_w[l], row(conv_b), wri, row(b_rgate), row(b_igate), row(lru_lambda))
    w1 = (w_ffn1_gate, w_ffn1_up, w_ffn1_down)
    w2 = (w_ffn2_gate, w_ffn2_up, w_ffn2_down)
    ffn_tiles = dict(tm=1024, tf=256)
    ffn_tiles_sample = dict(tm=512, tf=512)
    win = w_in[l].astype(BF16)
    wba = w_branch_attn[l].astype(BF16)
    wbr = w_branch_rec[l].astype(BF16)
    wout = w_out[l].astype(BF16)
    wpg = w_ple_gate[l].astype(BF16)
    wpp = w_ple_proj[l].astype(BF16)
    lnf = ln_final.reshape(1, -1)

    x_col = 3 * QKV_WIDTH
    y_col = x_col + d_rnn
    g_col = y_col + d_rnn
    tm = 512

    xp = _ffn(xp, row(ln_ffn1), *w1, **ffn_tiles)
    zp = _inproj(xp, row(ln_mix), win, cos_p, sin_p, tm=1024, tn=1024)
    zp3 = zp.reshape(batch, seq, -1)
    attn_p, kv1_p, kv2_p, kv3_p = _attn_prompt(zp3, batch, seq)
    rec_p, hlast_p = _rglru_prompt(zp3, x_col, y_col, rg_params, d_rnn=d_rnn, tc=256, ch=1024)
    conv_p = zp3[:, seq - (CONV_W - 1):, x_col:x_col + d_rnn][None]
    xp = _merge(xp, attn_p.reshape(batch * seq, -1), rec_p.reshape(batch * seq, -1), zp, g_col,
                wba, wbr, wout, tm=256)
    xp = _ffn(xp, row(ln_ffn2), *w2, **ffn_tiles)
    yp = _ple(xp, p_prompt[l].reshape(batch * seq, -1), row(ln_ple), wpg, wpp, lnf, tm=512)

    xs = _ffn(xs, row(ln_ffn1), *w1, **ffn_tiles_sample)
    zs = _inproj(xs, row(ln_mix), win, cos_s, sin_s, tm=tm, tn=512)
    qkv = zs[:, :3 * QKV_WIDTH].reshape(nb, nq, 3, N_GROUPS, HEADS_PER_GROUP, HEAD_DIM)
    qkv = qkv.transpose(2, 0, 3, 4, 1, 5)
    attn_s = _attn_sample(qkv[0], qkv[1], qkv[2], cache_kv_w128, cache_kv_w512, cache_kv_w2048)
    attn_s = attn_s.transpose(0, 2, 1, 3).reshape(nb * nq, -1)
    kv_s = [jnp.stack([qkv[1][:, g], qkv[2][:, g]], axis=1)[None] for g in range(N_GROUPS)]

    x_rec = zs[:, x_col:x_col + d_rnn].reshape(nb, nq, d_rnn)
    y_rec = zs[:, y_col:y_col + d_rnn].reshape(nb, nq, d_rnn)
    zero1 = jnp.zeros((nb, 1, d_rnn), F32)
    x8 = jnp.concatenate([zero1, state_conv[l], x_rec], axis=1).reshape(nb * SUBLANES, d_rnn)
    y8 = jnp.concatenate([jnp.zeros((nb, CONV_W, d_rnn), F32), y_rec], axis=1).reshape(nb * SUBLANES, d_rnn)
    h8 = jnp.concatenate([jnp.zeros((nb, CONV_W - 1, d_rnn), F32), state_rglru[l][:, None],
                          jnp.zeros((nb, nq, d_rnn), F32)], axis=1).reshape(nb * SUBLANES, d_rnn)
    rec8, hs8 = _rglru_sample(x8, y8, h8, rg_params, tc=256, ch=1024)
    rec_s = rec8.reshape(nb, SUBLANES, d_rnn)[:, CONV_W:].reshape(nb * nq, d_rnn)
    hlast_s = hs8.reshape(nb, SUBLANES, d_rnn)[:, SUBLANES - 1][None]
    conv_s = jnp.concatenate([state_conv[l], x_rec], axis=1)[:, -(CONV_W - 1):][None]
    xs = _merge(xs, attn_s, rec_s, zs, g_col, wba, wbr, wout, tm=256)
    xs = _ffn(xs, row(ln_ffn2), *w2, **ffn_tiles_sample)
    ys = _ple(xs, p_sample[l].reshape(nb * nq, -1), row(ln_ple), wpg, wpp, lnf, tm=512)

    return (yp.reshape(batch, seq, d), ys.reshape(nb, nq, d),
            kv1_p, kv2_p, kv3_p, conv_p, hlast_p.reshape(1, batch, d_rnn),
            kv_s[0], kv_s[1], kv_s[2], conv_s, hlast_s)
```

```python
import functools
import math

import jax
import jax.numpy as jnp
from jax import lax
from jax.experimental import pallas as pl
from jax.experimental.pallas import tpu as pltpu

F32 = jnp.float32
BF16 = jnp.bfloat16

HEAD_DIM = 128
HEADS_PER_GROUP = 8
ATTN_GROUPS = ((128, 1), (512, 4), (2048, 16))
N_GROUPS = 3
QKV_WIDTH = N_GROUPS * HEADS_PER_GROUP * HEAD_DIM
ATTN_SCALE = HEAD_DIM ** -0.5
BAND_BLOCK = 128
ROPE_THETA = 10000.0
LRU_BLOCK_W = 128
LRU_C = 8.0
CONV_W = 4
EPS = 1e-6

SUBLANES = 8
LANES = 128
VMEM_LIMIT = 48 * 1024 * 1024
FFN_VMEM_LIMIT = 56 * 1024 * 1024
NEG_INF = float("-inf")
PROMPT_BLOCKS_PER_TRIP = 16
SAMPLE_HEADS_PER_TRIP = 4
ROPE_ROW_BAND = 64
ROW_BAND = 256
SCAN_TILES_PER_PASS = 16


def _params(*semantics):
    return pltpu.CompilerParams(dimension_semantics=semantics, vmem_limit_bytes=VMEM_LIMIT)


def _rmsnorm(x, g):
    return x * lax.rsqrt(jnp.mean(x * x, axis=-1, keepdims=True) + EPS) * g


def _dot(a, b):
    return jnp.dot(a, b, preferred_element_type=F32)


def _dot_nt(a, b):
    return lax.dot_general(a, b, (((1,), (1,)), ((), ())), preferred_element_type=F32)


def _ffn_body(x_ref, ln_ref, wg_ref, wu_ref, wd_ref, o_ref, h_ref):
    f = pl.program_id(1)

    def half_swiglu(h):
        g = _dot(h, wg_ref[...].astype(BF16))
        u = _dot(h, wu_ref[...].astype(BF16))
        a = (0.5 * g * jax.nn.sigmoid(g)) * u
        return _dot(a.astype(BF16), wd_ref[...].astype(BF16))

    @pl.when(f == 0)
    def _():
        for r0 in range(0, x_ref.shape[0], ROW_BAND):
            rows = slice(r0, r0 + ROW_BAND)
            x = x_ref[rows, :]
            h = _rmsnorm(x, ln_ref[...]).astype(BF16)
            h_ref[rows, :] = h
            o_ref[rows, :] = x + half_swiglu(h)

    @pl.when(f > 0)
    def _():
        o_ref[...] += half_swiglu(h_ref[...])


def _ffn(x, ln, wg, wu, wd, *, tm, tf):
    m, d = x.shape
    tm = min(tm, m)
    ff = wg.shape[2]
    return pl.pallas_call(
        _ffn_body,
        grid=(m // tm, ff // tf),
        in_specs=[
            pl.BlockSpec((tm, d), lambda i, f: (i, 0)),
            pl.BlockSpec((1, d), lambda i, f: (0, 0)),
            pl.BlockSpec((None, d, tf), lambda i, f: (0, 0, f)),
            pl.BlockSpec((None, d, tf), lambda i, f: (0, 0, f)),
            pl.BlockSpec((None, tf, d), lambda i, f: (0, f, 0)),
        ],
        out_specs=pl.BlockSpec((tm, d), lambda i, f: (i, 0)),
        out_shape=jax.ShapeDtypeStruct((m, d), F32),
        scratch_shapes=[pltpu.VMEM((tm, d), BF16)],
        compiler_params=pltpu.CompilerParams(dimension_semantics=("parallel", "arbitrary"),
                                             vmem_limit_bytes=FFN_VMEM_LIMIT),
        name="ffn",
    )(x, ln, wg, wu, wd)


def _inproj_body(x_ref, ln_ref, w_ref, cos_ref, sin_ref, z_ref, h_ref, zb0, zb1, *,
                 rope_tiles, col_tiles):
    j = pl.program_id(1)
    bufs = (zb0, zb1)
    assert 0 < rope_tiles < col_tiles

    def finish_rope(src):
        for r0 in range(0, z_ref.shape[0], ROPE_ROW_BAND):
            rows = slice(r0, r0 + ROPE_ROW_BAND)
            cos = cos_ref[rows, :]
            sin = sin_ref[rows, :]
            for c in range(z_ref.shape[1] // HEAD_DIM):
                cols = slice(c * HEAD_DIM, (c + 1) * HEAD_DIM)
                zc = src[rows, cols]
                z_ref[rows, cols] = zc * cos + pltpu.roll(zc, HEAD_DIM // 2, axis=1) * sin

    @pl.when(j == 0)
    def _():
        h_ref[...] = _rmsnorm(x_ref[...], ln_ref[...]).astype(BF16)
        zb0[...] = _dot(h_ref[...], w_ref[...])

    for parity in (0, 1):
        cur, prev = bufs[parity], bufs[1 - parity]
        mine = (j % 2) == parity

        @pl.when(mine & (j >= 1) & (j < rope_tiles))
        def _(cur=cur, prev=prev):
            cur[...] = _dot(h_ref[...], w_ref[...])
            finish_rope(prev)

    @pl.when(j == rope_tiles)
    def _():
        finish_rope(bufs[(rope_tiles - 1) % 2])

    @pl.when(j > rope_tiles)
    def _():
        z_ref[...] = _dot(h_ref[...], w_ref[...])


def _inproj(x, ln, w, cos, sin, *, tm, tn):
    m, d = x.shape
    tm = min(tm, m)
    n = w.shape[1]
    col_tiles = n // tn
    rope_tiles = 2 * QKV_WIDTH // tn
    pos_tiles = cos.shape[0] // tm
    return pl.pallas_call(
        functools.partial(_inproj_body, rope_tiles=rope_tiles, col_tiles=col_tiles),
        grid=(m // tm, col_tiles + 1),
        in_specs=[
            pl.BlockSpec((tm, d), lambda i, j: (i, 0)),
            pl.BlockSpec((1, d), lambda i, j: (0, 0)),
            pl.BlockSpec((d, tn), lambda i, j: (0, jnp.where(j < rope_tiles, j, j - 1))),
            pl.BlockSpec((tm, HEAD_DIM), lambda i, j: (i % pos_tiles, 0)),
            pl.BlockSpec((tm, HEAD_DIM), lambda i, j: (i % pos_tiles, 0)),
        ],
        out_specs=pl.BlockSpec((tm, tn), lambda i, j: (i, jnp.maximum(j - 1, 0))),
        out_shape=jax.ShapeDtypeStruct((m, n), F32),
        scratch_shapes=[pltpu.VMEM((tm, d), BF16), pltpu.VMEM((tm, tn), F32),
                        pltpu.VMEM((tm, tn), F32)],
        compiler_params=_params("parallel", "arbitrary"),
        name="inproj",
    )(x, ln, w, cos, sin)


def _rope_tables(pos):
    half = HEAD_DIM // 2
    inv_freq = jnp.exp(-math.log(ROPE_THETA) * jnp.arange(half, dtype=F32) / half)
    ang = pos.astype(F32)[:, None] * inv_freq[None, :]
    cos, sin = jnp.cos(ang), jnp.sin(ang)
    return jnp.concatenate([cos, cos], axis=-1), jnp.concatenate([-sin, sin], axis=-1)


def _attn_prompt_body(q1, q2, q3, k1, k2, k3, v1, v2, v3,
                      attn_ref, kv128_ref, kv512_ref, kv2048_ref,
                      o_s, m_s, l_s):
    seq = q1.shape[0]
    blk = BAND_BLOCK
    for kv_ref, k_ref, v_ref in ((kv128_ref, k1, v1), (kv512_ref, k2, v2), (kv2048_ref, k3, v3)):
        keep = kv_ref.shape[1]
        kv_ref[0] = k_ref[pl.ds(seq - keep, keep), :]
        kv_ref[1] = v_ref[pl.ds(seq - keep, keep), :]

    row = lax.broadcasted_iota(jnp.int32, (blk, blk), 0)
    col = lax.broadcasted_iota(jnp.int32, (blk, blk), 1)
    own_mask = col <= row
    prev_mask = col >= row
    ones = jnp.ones((blk, HEAD_DIM), BF16)

    groups = ((q1, k1, v1), (q2, k2, v2), (q3, k3, v3))
    for g, ((q_ref, k_ref, v_ref), (_, dil)) in enumerate(zip(groups, ATTN_GROUPS)):
        nblk = seq // dil // blk
        has_prev = nblk > 1

        def rows(ref, start, dil=dil):
            if dil == 1:
                return ref[pl.ds(start, blk), :]
            return ref[pl.ds(start, blk, stride=dil), :]

        def sub(it, carry, g=g, dil=dil, nblk=nblk, has_prev=has_prev,
                q_ref=q_ref, k_ref=k_ref, v_ref=v_ref, rows=rows):
            work = []
            for u in range(PROMPT_BLOCKS_PER_TRIP):
                p = it * PROMPT_BLOCKS_PER_TRIP + u
                w = {"r": p // nblk, "n": p % nblk}
                w["qs"] = w["r"] + dil * (w["n"] * blk)
                w["q"] = rows(q_ref, w["qs"]).astype(BF16)
                w["kc"] = rows(k_ref, w["qs"]).astype(BF16)
                if has_prev:
                    w["ps"] = w["r"] + dil * (jnp.maximum(w["n"] - 1, 0) * blk)
                    w["kp"] = rows(k_ref, w["ps"]).astype(BF16)
                work.append(w)
            for w in work:
                w["s_own"] = jnp.where(own_mask, _dot_nt(w["q"], w["kc"]) * ATTN_SCALE, NEG_INF)
                if has_prev:
                    pm = jnp.logical_and(prev_mask, w["n"] > 0)
                    w["s_prev"] = jnp.where(pm, _dot_nt(w["q"], w["kp"]) * ATTN_SCALE, NEG_INF)
            for w in work:
                m = jnp.max(w["s_own"], axis=-1, keepdims=True)
                if has_prev:
                    m = jnp.maximum(m, jnp.max(w["s_prev"], axis=-1, keepdims=True))
                w["m"] = m
                w["e_own"] = jnp.exp(w["s_own"] - m).astype(BF16)
                if has_prev:
                    w["e_prev"] = jnp.exp(w["s_prev"] - m).astype(BF16)
            for w in work:
                acc = _dot(w["e_own"], rows(v_ref, w["qs"]).astype(BF16))
                l = _dot(w["e_own"], ones)
                if has_prev:
                    acc = acc + _dot(w["e_prev"], rows(v_ref, w["ps"]).astype(BF16))
                    l = l + _dot(w["e_prev"], ones)
                w["acc"] = acc
                w["l"] = l
            for w in work:
                if dil == 1:
                    idx = pl.ds(w["qs"], blk)
                else:
                    idx = pl.ds(w["qs"], blk, stride=dil)
                o_s[g, idx, :] = w["acc"]
                m_s[g, idx, :] = jnp.broadcast_to(w["m"], (blk, HEAD_DIM))
                l_s[g, idx, :] = w["l"]
            return carry

        lax.fori_loop(0, dil * nblk // PROMPT_BLOCKS_PER_TRIP, sub, 0)

    def combine(t, carry):
        idx = pl.ds(pl.multiple_of(t * blk, blk), blk)
        ms = [m_s[g, idx, :] for g in range(N_GROUPS)]
        mx = jnp.maximum(jnp.maximum(ms[0], ms[1]), ms[2])
        num = jnp.zeros((blk, HEAD_DIM), F32)
        den = jnp.zeros((blk, HEAD_DIM), F32)
        for g in range(N_GROUPS):
            w = jnp.exp(ms[g] - mx)
            num = num + w * o_s[g, idx, :]
            den = den + w * l_s[g, idx, :]
        attn_ref[idx, :] = (num / den).astype(attn_ref.dtype)
        return carry

    lax.fori_loop(0, seq // blk, combine, 0)


def _attn_prompt(z3, batch, seq):
    hd = HEAD_DIM
    nh = HEADS_PER_GROUP

    def col_spec(base):
        return pl.BlockSpec((None, seq, hd), lambda b, h, base=base: (b, 0, base + h))

    in_specs = [col_spec((t * N_GROUPS + g) * nh) for t in range(3) for g in range(N_GROUPS)]
    keeps = [min(w, seq) for w, _ in ATTN_GROUPS]
    kv_shapes = [jax.ShapeDtypeStruct((1, batch, 2, nh, keep, hd), F32) for keep in keeps]
    kv_specs = [pl.BlockSpec((None, None, 2, None, keep, hd), lambda b, h: (0, b, 0, h, 0, 0))
                for keep in keeps]
    return pl.pallas_call(
        _attn_prompt_body,
        grid=(batch, nh),
        in_specs=in_specs,
        out_specs=[pl.BlockSpec((None, seq, hd), lambda b, h: (b, 0, h))] + kv_specs,
        out_shape=[jax.ShapeDtypeStruct((batch, seq, nh * hd), BF16)] + kv_shapes,
        scratch_shapes=[pltpu.VMEM((N_GROUPS, seq, hd), F32) for _ in range(3)],
        compiler_params=_params("parallel", "parallel"),
        name="attn_prompt",
    )(*([z3] * 9))


def _attn_sample_body(q_ref, k_ref, v_ref, c1_ref, c2_ref, c3_hbm, attn_ref, c3_buf, c3_sem, *,
                      seq_id, num_seqs):
    nq = q_ref.shape[2]
    hd = HEAD_DIM
    pad = SUBLANES - nq

    b = seq_id
    slot = b % 2

    def c3_copy(seq, s):
        return pltpu.make_async_copy(c3_hbm.at[0, seq, :, :, :, pl.ds(0, nq), :],
                                     c3_buf.at[s, :, :, :, pl.ds(0, nq), :], c3_sem.at[s])

    @pl.when(b == 0)
    def _():
        c3_buf[:, :, :, :, pl.ds(nq, pad), :] = jnp.zeros(c3_buf.shape[:4] + (pad, hd), F32)
        c3_copy(0, 0).start()

    @pl.when(b + 1 < num_seqs)
    def _():
        c3_copy(b + 1, 1 - slot).start()

    c3_copy(b, slot).wait()
    caches = (c1_ref, c2_ref, c3_buf.at[slot])

    def cache_rows(g, kv, h):
        x = caches[g][kv, h]
        return x.reshape(-1, hd).astype(BF16)

    def mask_for(g):
        n = caches[g].shape[2] * (caches[g].shape[3] if g == 2 else 1)
        row = lax.broadcasted_iota(jnp.int32, (SUBLANES, n), 0)
        col = lax.broadcasted_iota(jnp.int32, (SUBLANES, n), 1)
        if g == 0:
            return col >= row
        step = ATTN_GROUPS[1][1] if g == 1 else caches[g].shape[3]
        assert step & (step - 1) == 0
        return jnp.bitwise_and(col, step - 1) == row

    masks = [mask_for(g) for g in range(N_GROUPS)]
    qrow = lax.broadcasted_iota(jnp.int32, (SUBLANES, 1), 0)

    def scores(h):
        parts = []
        news = []
        for g in range(N_GROUPS):
            q = jnp.concatenate([q_ref[g, h], jnp.zeros((pad, hd), F32)], axis=0).astype(BF16)
            s = _dot_nt(q, cache_rows(g, 0, h)) * ATTN_SCALE
            parts.append((jnp.where(masks[g], s, NEG_INF), g))
            qf = q.astype(F32)
            kn = k_ref[g, h].astype(BF16).astype(F32)
            vn = v_ref[g, h].astype(BF16).astype(F32)
            if g == 0:
                for c in range(nq):
                    sn = jnp.sum(qf * kn[c:c + 1, :], axis=-1, keepdims=True) * ATTN_SCALE
                    news.append((jnp.where(qrow >= c, sn, NEG_INF), vn[c:c + 1, :]))
            else:
                knp = jnp.concatenate([kn, jnp.zeros((pad, hd), F32)], axis=0)
                sn = jnp.sum(qf * knp, axis=-1, keepdims=True) * ATTN_SCALE
                for c in range(nq):
                    news.append((jnp.where(qrow == c, sn, NEG_INF), vn[c:c + 1, :]))
        return parts, news

    def softmax(parts, news):
        m = parts[0][0].max(axis=-1, keepdims=True)
        for s, _ in parts[1:]:
            m = jnp.maximum(m, s.max(axis=-1, keepdims=True))
        for s, _ in news:
            m = jnp.maximum(m, s)
        es = [jnp.exp(s - m) for s, _ in parts]
        l = jnp.zeros((SUBLANES, 1), F32)
        for e in es:
            l = l + jnp.sum(e, axis=-1, keepdims=True)
        acc = jnp.zeros((SUBLANES, hd), F32)
        for s, v in news:
            e = jnp.exp(s - m)
            l = l + e
            acc = acc + e.astype(BF16).astype(F32) * v
        return es, l, acc

    def trip(it, carry):
        heads = [it * SAMPLE_HEADS_PER_TRIP + u for u in range(SAMPLE_HEADS_PER_TRIP)]
        staged = [scores(h) for h in heads]
        soft = [softmax(parts, news) for parts, news in staged]
        outs = []
        for h, (parts, _), (es, l, acc) in zip(heads, staged, soft):
            for e, (_, g) in zip(es, parts):
                acc = acc + _dot(e.astype(BF16), cache_rows(g, 1, h))
            outs.append((acc / l)[:nq, :])
        for h, out in zip(heads, outs):
            attn_ref[h] = out
        return carry

    lax.fori_loop(0, HEADS_PER_GROUP // SAMPLE_HEADS_PER_TRIP, trip, 0)


def _attn_sample_specs(q_s, c128, c512, c2048, seq_of):
    nb, _, nh, nq, hd = q_s.shape
    new_spec = pl.BlockSpec((None, N_GROUPS, nh, nq, hd), lambda *g: (seq_of(*g), 0, 0, 0, 0))
    for cache, (window, _) in zip((c128, c512, c2048), ATTN_GROUPS):
        assert cache.shape[4] == window, "the caches must hold a full window"
    assert nq <= ATTN_GROUPS[1][1]
    dil3 = ATTN_GROUPS[2][1]
    rows3 = c2048.shape[4] // dil3
    c2048 = c2048.reshape(1, nb, 2, nh, rows3, dil3, hd)
    in_specs = [
        new_spec, new_spec, new_spec,
        pl.BlockSpec((None, None, 2, nh, c128.shape[4], hd), lambda *g: (0, seq_of(*g), 0, 0, 0, 0)),
        pl.BlockSpec((None, None, 2, nh, c512.shape[4], hd), lambda *g: (0, seq_of(*g), 0, 0, 0, 0)),
        pl.BlockSpec(memory_space=pl.ANY),
    ]
    out_spec = pl.BlockSpec((None, nh, nq, hd), lambda *g: (seq_of(*g), 0, 0, 0))
    out_shape = jax.ShapeDtypeStruct((nb, nh, nq, hd), F32)
    scratch = [pltpu.VMEM((2, 2, nh, rows3, SUBLANES, hd), F32), pltpu.SemaphoreType.DMA((2,))]
    return c2048, in_specs, out_spec, out_shape, scratch


def _rglru_body(*refs, chained):
    if chained:
        (x_ref, y_ref, cw_ref, cb_ref, wri_ref, br_ref, bi_ref, lam_ref,
         rec_ref, hout_ref, sa, sb, sh, tail_ref, carry_ref) = refs
    else:
        (x_ref, y_ref, h0_ref, cw_ref, cb_ref, wri_ref, br_ref, bi_ref, lam_ref,
         rec_ref, hout_ref, sa, sb, sh) = refs
    tc, ch = x_ref.shape
    tiles = (tc // SUBLANES, SUBLANES, ch)
    step = pl.program_id(2) if chained else None
    row8 = lax.broadcasted_iota(jnp.int32, tiles, 1)

    cur = x_ref[...]
    if chained:
        @pl.when(step == 0)
        def _():
            tail_ref[...] = jnp.zeros_like(tail_ref)
            carry_ref[...] = jnp.zeros_like(carry_ref)

        prev = jnp.concatenate([tail_ref[...], cur[:tc - SUBLANES, :]], axis=0)
        tail_ref[...] = cur[tc - SUBLANES:, :]
    else:
        prev = cur
    cur = cur.reshape(tiles)
    prev = prev.reshape(tiles)

    xc = cb_ref[...]
    for j in range(CONV_W):
        d = CONV_W - 1 - j
        xs = cur if d == 0 else pltpu.roll(jnp.where(row8 >= SUBLANES - d, prev, cur), d, axis=1)
        xc = xc + xs * cw_ref[j:j + 1, :]

    xc2 = xc.reshape(tc, ch)
    pre_r, pre_i = [], []
    for n in range(ch // LRU_BLOCK_W):
        blk = xc2[:, n * LRU_BLOCK_W:(n + 1) * LRU_BLOCK_W].astype(BF16)
        ri = _dot(blk, wri_ref[n])
        pre_r.append(ri[:, :LRU_BLOCK_W])
        pre_i.append(ri[:, LRU_BLOCK_W:])
    r = jax.nn.sigmoid(jnp.concatenate(pre_r, axis=-1) + br_ref[...])
    i = jax.nn.sigmoid(jnp.concatenate(pre_i, axis=-1) + bi_ref[...])
    log_a = r * (-LRU_C * jax.nn.softplus(-lam_ref[...]))
    a = jnp.exp(log_a)
    th = jnp.tanh(log_a)
    p = -2.0 * th
    w = 1.0 - th
    mult = jnp.where(p > 0.0, p * lax.rsqrt(p * w), 0.0)
    b = mult * (i * xc2)

    if not chained:
        a = jnp.where(row8 < CONV_W, 1.0, a.reshape(tiles)).reshape(tc, ch)
        b = jnp.where(row8 < CONV_W, h0_ref[...].reshape(tiles), b.reshape(tiles)).reshape(tc, ch)

    slabs = ch // LANES
    for c in range(slabs):
        sa[c] = a[:, c * LANES:(c + 1) * LANES]
        sb[c] = b[:, c * LANES:(c + 1) * LANES]
    span = SCAN_TILES_PER_PASS
    tile_row = lax.broadcasted_iota(jnp.int32, (span, LANES), 0)
    for c in range(slabs):
        carry = carry_ref[c, 0:1, :] if chained else None
        for t0 in range(0, tc, span * SUBLANES):
            rows = [pl.ds(t0 + j, span, stride=SUBLANES) for j in range(SUBLANES)]
            a_rows = [sa[c, idx, :] for idx in rows]
            b_rows = [sb[c, idx, :] for idx in rows]
            acc_a, acc_b = [a_rows[0]], [b_rows[0]]
            for j in range(1, SUBLANES):
                acc_b.append(a_rows[j] * acc_b[-1] + b_rows[j])
                acc_a.append(a_rows[j] * acc_a[-1])
            if chained:
                ta, tb = acc_a[-1], acc_b[-1]
                shift = 1
                while shift < span:
                    valid = tile_row >= shift
                    tb = jnp.where(valid, ta * pltpu.roll(tb, shift, axis=0) + tb, tb)
                    ta = jnp.where(valid, ta * pltpu.roll(ta, shift, axis=0), ta)
                    shift *= 2
                state_out = ta * carry + tb
                state_in = jnp.where(tile_row == 0, carry, pltpu.roll(state_out, 1, axis=0))
                acc_b = [pa * state_in + pb for pa, pb in zip(acc_a, acc_b)]
                carry = state_out[span - 1:span, :]
            for j, idx in enumerate(rows):
                sh[c, idx, :] = acc_b[j]
        if chained:
            carry_ref[c] = jnp.broadcast_to(carry, (SUBLANES, LANES))
    h = jnp.concatenate([sh[c] for c in range(slabs)], axis=-1)

    rec_ref[...] = (h * jax.nn.gelu(y_ref[...])).astype(rec_ref.dtype)
    if chained:
        @pl.when(step == pl.num_programs(2) - 1)
        def _():
            hout_ref[...] = h[tc - 1:tc, :]
    else:
        hout_ref[...] = h


def _rglru_params_specs(ch, idx):
    nblk = ch // LRU_BLOCK_W
    return [
        pl.BlockSpec((CONV_W, ch), lambda *g: (0, idx(*g))),
        pl.BlockSpec((1, ch), lambda *g: (0, idx(*g))),
        pl.BlockSpec((nblk, LRU_BLOCK_W, 2 * LRU_BLOCK_W), lambda *g: (idx(*g), 0, 0)),
        pl.BlockSpec((1, ch), lambda *g: (0, idx(*g))),
        pl.BlockSpec((1, ch), lambda *g: (0, idx(*g))),
        pl.BlockSpec((1, ch), lambda *g: (0, idx(*g))),
    ]


N_RGLRU_IN = 8
N_RGLRU_SCRATCH = 5


def _rglru_prompt_attn_sample_body(*refs):
    rg_in, rest = refs[:N_RGLRU_IN], refs[N_RGLRU_IN:]
    at_in, rest = rest[:6], rest[6:]
    (rec_ref, hout_ref, attn_ref), rest = rest[:3], rest[3:]
    rg_scratch, at_scratch = rest[:N_RGLRU_SCRATCH], rest[N_RGLRU_SCRATCH:]
    _rglru_body(*rg_in, rec_ref, hout_ref, *rg_scratch, chained=True)
    n1, n2 = pl.num_programs(1), pl.num_programs(2)
    seq_id = (pl.program_id(0) * n1 + pl.program_id(1)) * n2 + pl.program_id(2)
    _attn_sample_body(*at_in, attn_ref, *at_scratch,
                      seq_id=seq_id, num_seqs=pl.num_programs(0) * n1 * n2)


def _rglru_prompt_attn_sample(z3, x_col, y_col, params, q_s, k_s, v_s, c128, c512, c2048,
                              *, d_rnn, tc, ch):
    batch, seq, _ = z3.shape
    nc = d_rnn // ch
    nt = seq // tc
    assert q_s.shape[0] == batch * nc * nt, "one sample sequence per recurrence step"
    c2048, at_specs, at_out_spec, at_out_shape, at_scratch = _attn_sample_specs(
        q_s, c128, c512, c2048, lambda b, c, t: (b * nc + c) * nt + t)
    rec, hlast, attn = pl.pallas_call(
        _rglru_prompt_attn_sample_body,
        grid=(batch, nc, nt),
        in_specs=[
            pl.BlockSpec((None, tc, ch), lambda b, c, t: (b, t, x_col // ch + c)),
            pl.BlockSpec((None, tc, ch), lambda b, c, t: (b, t, y_col // ch + c)),
        ] + _rglru_params_specs(ch, lambda b, c, t: c) + at_specs,
        out_specs=[
            pl.BlockSpec((None, tc, ch), lambda b, c, t: (b, t, c)),
            pl.BlockSpec((None, 1, ch), lambda b, c, t: (b, 0, c)),
            at_out_spec,
        ],
        out_shape=[jax.ShapeDtypeStruct((batch, seq, d_rnn), BF16),
                   jax.ShapeDtypeStruct((batch, 1, d_rnn), F32),
                   at_out_shape],
        scratch_shapes=[pltpu.VMEM((ch // LANES, tc, LANES), F32) for _ in range(3)]
        + [pltpu.VMEM((SUBLANES, ch), F32), pltpu.VMEM((ch // LANES, SUBLANES, LANES), F32)]
        + at_scratch,
        compiler_params=_params("arbitrary", "arbitrary", "arbitrary"),
        name="rglru_prompt_attn_sample",
    )(z3, z3, *params, q_s, k_s, v_s, c128, c512, c2048)
    return rec, hlast, attn


def _rglru_sample(x8, y8, h8, params, *, tc, ch):
    rows, d_rnn = x8.shape
    nc = d_rnn // ch
    row_spec = pl.BlockSpec((tc, ch), lambda c, t: (t, c))
    return pl.pallas_call(
        functools.partial(_rglru_body, chained=False),
        grid=(nc, rows // tc),
        in_specs=[row_spec, row_spec, row_spec] + _rglru_params_specs(ch, lambda c, t: c),
        out_specs=[row_spec, row_spec],
        out_shape=[jax.ShapeDtypeStruct((rows, d_rnn), F32),
                   jax.ShapeDtypeStruct((rows, d_rnn), F32)],
        scratch_shapes=[pltpu.VMEM((ch // LANES, tc, LANES), F32) for _ in range(3)],
        compiler_params=_params("parallel", "parallel"),
        name="rglru_sample",
    )(x8, y8, h8, *params)


def _merge_body(x_ref, attn_ref, rec_ref, ga0_ref, ga1_ref, gr0_ref, gr1_ref,
                wba_ref, wbr_ref, wout_ref, o_ref):
    pa = _dot(attn_ref[...].astype(BF16), wba_ref[...])
    pr = _dot(rec_ref[...].astype(BF16), wbr_ref[...])
    ga = jnp.concatenate([ga0_ref[...], ga1_ref[...]], axis=-1)
    gr = jnp.concatenate([gr0_ref[...], gr1_ref[...]], axis=-1)
    merged = jax.nn.sigmoid(ga) * pa + jax.nn.sigmoid(gr) * pr
    o_ref[...] = x_ref[...] + _dot(merged.astype(BF16), wout_ref[...])


def _merge(x, attn, rec, z, g_col, wba, wbr, wout, *, tm):
    m, d = x.shape
    tm = min(tm, m)
    half = d // 2
    gb = g_col // half

    def whole(w):
        return pl.BlockSpec(w.shape, lambda i: (0, 0), pipeline_mode=pl.Buffered(1))

    def gate_spec(k):
        return pl.BlockSpec((tm, half), lambda i, k=k: (i, gb + k))

    return pl.pallas_call(
        _merge_body,
        grid=(m // tm,),
        in_specs=[
            pl.BlockSpec((tm, d), lambda i: (i, 0)),
            pl.BlockSpec((tm, attn.shape[1]), lambda i: (i, 0)),
            pl.BlockSpec((tm, rec.shape[1]), lambda i: (i, 0)),
            gate_spec(0), gate_spec(1), gate_spec(2), gate_spec(3),
            whole(wba), whole(wbr), whole(wout),
        ],
        out_specs=pl.BlockSpec((tm, d), lambda i: (i, 0)),
        out_shape=jax.ShapeDtypeStruct((m, d), F32),
        compiler_params=_params("parallel"),
        name="merge",
    )(x, attn, rec, z, z, z, z, wba, wbr, wout)


def _ple_body(x_ref, pe_ref, lnp_ref, wpg_ref, wpp_ref, lnf_ref, y_ref):
    for r0 in range(0, x_ref.shape[0], ROW_BAND):
        rows = slice(r0, r0 + ROW_BAND)
        x = x_ref[rows, :]
        hn = _rmsnorm(x, lnp_ref[...]).astype(BF16)
        gate = jax.nn.sigmoid(_dot(hn, wpg_ref[...]))
        proj = _dot(pe_ref[rows, :].astype(BF16), wpp_ref[...])
        y_ref[rows, :] = _rmsnorm(x + gate * proj, lnf_ref[...])


def _ple(x, pe, lnp, wpg, wpp, lnf, *, tm):
    m, d = x.shape
    tm = min(tm, m)

    def whole(w):
        return pl.BlockSpec(w.shape, lambda i: (0, 0), pipeline_mode=pl.Buffered(1))

    return pl.pallas_call(
        _ple_body,
        grid=(m // tm,),
        in_specs=[
            pl.BlockSpec((tm, d), lambda i: (i, 0)),
            pl.BlockSpec((tm, pe.shape[1]), lambda i: (i, 0)),
            pl.BlockSpec((1, d), lambda i: (0, 0)),
            whole(wpg), whole(wpp),
            pl.BlockSpec((1, d), lambda i: (0, 0)),
        ],
        out_specs=pl.BlockSpec((tm, d), lambda i: (i, 0)),
        out_shape=jax.ShapeDtypeStruct((m, d), F32),
        compiler_params=_params("parallel"),
        name="ple",
    )(x, pe, lnp, wpg, wpp, lnf)


def kernel(x_prompt, x_sample, cache_kv_w128, cache_kv_w512, cache_kv_w2048, state_conv, state_rglru, p_prompt, p_sample, ln_ffn1, w_ffn1_gate, w_ffn1_up, w_ffn1_down, ln_mix, w_in, conv_w, conv_b, w_rgate, b_rgate, w_igate, b_igate, lru_lambda, w_branch_attn, w_branch_rec, w_out, ln_ffn2, w_ffn2_gate, w_ffn2_up, w_ffn2_down, ln_ple, w_ple_gate, w_ple_proj, ln_final):
    batch, seq, d = x_prompt.shape
    nb, nq, _ = x_sample.shape
    past_len = cache_kv_w2048.shape[4]
    d_rnn = conv_w.shape[-1]
    depth = w_in.shape[0]
    assert depth == 1

    xp = x_prompt.reshape(batch * seq, d)
    xs = x_sample.reshape(nb * nq, d)
    cos_p, sin_p = _rope_tables(jnp.arange(seq))
    cos_s, sin_s = _rope_tables(past_len + (jnp.arange(nb * nq) % nq))

    l = 0
    row = lambda v: v[l].reshape(1, -1)
    wri = jnp.concatenate([w_rgate[l], w_igate[l]], axis=-1).astype(BF16)
    rg_params = (conv_w[l], row(conv_b), wri, row(b_rgate), row(b_igate), row(lru_lambda))
    w1 = (w_ffn1_gate, w_ffn1_up, w_ffn1_down)
    w2 = (w_ffn2_gate, w_ffn2_up, w_ffn2_down)
    ffn_tiles = dict(tm=1024, tf=256)
    ffn_tiles_sample = dict(tm=512, tf=512)
    win = w_in[l].astype(BF16)
    wba = w_branch_attn[l].astype(BF16)
    wbr = w_branch_rec[l].astype(BF16)
    wout = w_out[l].astype(BF16)
    wpg = w_ple_gate[l].astype(BF16)
    wpp = w_ple_proj[l].astype(BF16)
    lnf = ln_final.reshape(1, -1)

    x_col = 3 * QKV_WIDTH
    y_col = x_col + d_rnn
    g_col = y_col + d_rnn
    tm = 512

    xs = _ffn(xs, row(ln_ffn1), *w1, **ffn_tiles_sample)
    zs = _inproj(xs, row(ln_mix), win, cos_s, sin_s, tm=tm, tn=512)
    qkv = zs[:, :3 * QKV_WIDTH].reshape(nb, nq, 3, N_GROUPS, HEADS_PER_GROUP, HEAD_DIM)
    qkv = qkv.transpose(2, 0, 3, 4, 1, 5)

    xp = _ffn(xp, row(ln_ffn1), *w1, **ffn_tiles)
    zp = _inproj(xp, row(ln_mix), win, cos_p, sin_p, tm=1024, tn=1024)
    zp3 = zp.reshape(batch, seq, -1)
    attn_p, kv1_p, kv2_p, kv3_p = _attn_prompt(zp3, batch, seq)
    rec_p, hlast_p, attn_s = _rglru_prompt_attn_sample(
        zp3, x_col, y_col, rg_params, qkv[0], qkv[1], qkv[2],
        cache_kv_w128, cache_kv_w512, cache_kv_w2048, d_rnn=d_rnn, tc=256, ch=1024)
    conv_p = zp3[:, seq - (CONV_W - 1):, x_col:x_col + d_rnn][None]
    xp = _merge(xp, attn_p.reshape(batch * seq, -1), rec_p.reshape(batch * seq, -1), zp, g_col,
                wba, wbr, wout, tm=256)
    xp = _ffn(xp, row(ln_ffn2), *w2, **ffn_tiles)
    yp = _ple(xp, p_prompt[l].reshape(batch * seq, -1), row(ln_ple), wpg, wpp, lnf, tm=512)

    attn_s = attn_s.transpose(0, 2, 1, 3).reshape(nb * nq, -1)
    kv_s = [jnp.stack([qkv[1][:, g], qkv[2][:, g]], axis=1)[None] for g in range(N_GROUPS)]

    x_rec = zs[:, x_col:x_col + d_rnn].reshape(nb, nq, d_rnn)
    y_rec = zs[:, y_col:y_col + d_rnn].reshape(nb, nq, d_rnn)
    zero1 = jnp.zeros((nb, 1, d_rnn), F32)
    x8 = jnp.concatenate([zero1, state_conv[l], x_rec], axis=1).reshape(nb * SUBLANES, d_rnn)
    y8 = jnp.concatenate([jnp.zeros((nb, CONV_W, d_rnn), F32), y_rec], axis=1).reshape(nb * SUBLANES, d_rnn)
    h8 = jnp.concatenate([jnp.zeros((nb, CONV_W - 1, d_rnn), F32), state_rglru[l][:, None],
                          jnp.zeros((nb, nq, d_rnn), F32)], axis=1).reshape(nb * SUBLANES, d_rnn)
    rec8, hs8 = _rglru_sample(x8, y8, h8, rg_params, tc=256, ch=1024)
    rec_s = rec8.reshape(nb, SUBLANES, d_rnn)[:, CONV_W:].reshape(nb * nq, d_rnn)
    hlast_s = hs8.reshape(nb, SUBLANES, d_rnn)[:, SUBLANES - 1][None]
    conv_s = jnp.concatenate([state_conv[l], x_rec], axis=1)[:, -(CONV_W - 1):][None]
    xs = _merge(xs, attn_s, rec_s, zs, g_col, wba, wbr, wout, tm=256)
    xs = _ffn(xs, row(ln_ffn2), *w2, **ffn_tiles_sample)
    ys = _ple(xs, p_sample[l].reshape(nb * nq, -1), row(ln_ple), wpg, wpp, lnf, tm=512)

    return (yp.reshape(batch, seq, d), ys.reshape(nb, nq, d),
            kv1_p, kv2_p, kv3_p, conv_p, hlast_p.reshape(1, batch, d_rnn),
            kv_s[0], kv_s[1], kv_s[2], conv_s, hlast_s)
```

```python
import functools
import math

import jax
import jax.numpy as jnp
from jax import lax
from jax.experimental import pallas as pl
from jax.experimental.pallas import tpu as pltpu

F32 = jnp.float32
BF16 = jnp.bfloat16

HEAD_DIM = 128
HEADS_PER_GROUP = 8
ATTN_GROUPS = ((128, 1), (512, 4), (2048, 16))
N_GROUPS = 3
QKV_WIDTH = N_GROUPS * HEADS_PER_GROUP * HEAD_DIM
ATTN_SCALE = HEAD_DIM ** -0.5
BAND_BLOCK = 128
ROPE_THETA = 10000.0
LRU_BLOCK_W = 128
LRU_C = 8.0
CONV_W = 4
EPS = 1e-6

SUBLANES = 8
LANES = 128
VMEM_LIMIT = 48 * 1024 * 1024
FFN_VMEM_LIMIT = 56 * 1024 * 1024
NEG_INF = float("-inf")
PROMPT_BLOCKS_PER_TRIP = 16
SAMPLE_HEADS_PER_TRIP = 8
ROPE_ROW_BAND = 64
ROW_BAND = 256
SCAN_TILES_PER_PASS = 16


def _params(*semantics):
    return pltpu.CompilerParams(dimension_semantics=semantics, vmem_limit_bytes=VMEM_LIMIT)


def _rmsnorm(x, g):
    return x * lax.rsqrt(jnp.mean(x * x, axis=-1, keepdims=True) + EPS) * g


def _dot(a, b):
    return jnp.dot(a, b, preferred_element_type=F32)


def _dot_nt(a, b):
    return lax.dot_general(a, b, (((1,), (1,)), ((), ())), preferred_element_type=F32)


def _ffn_body(x_ref, ln_ref, wg_ref, wu_ref, wd_ref, o_ref, h_ref):
    f = pl.program_id(1)

    def half_swiglu(h):
        g = _dot(h, wg_ref[...].astype(BF16))
        u = _dot(h, wu_ref[...].astype(BF16))
        a = (0.5 * g * jax.nn.sigmoid(g)) * u
        return _dot(a.astype(BF16), wd_ref[...].astype(BF16))

    @pl.when(f == 0)
    def _():
        for r0 in range(0, x_ref.shape[0], ROW_BAND):
            rows = slice(r0, r0 + ROW_BAND)
            x = x_ref[rows, :]
            h = _rmsnorm(x, ln_ref[...]).astype(BF16)
            h_ref[rows, :] = h
            o_ref[rows, :] = x + half_swiglu(h)

    @pl.when(f > 0)
    def _():
        o_ref[...] += half_swiglu(h_ref[...])


def _ffn(x, ln, wg, wu, wd, *, tm, tf):
    m, d = x.shape
    tm = min(tm, m)
    ff = wg.shape[2]
    return pl.pallas_call(
        _ffn_body,
        grid=(m // tm, ff // tf),
        in_specs=[
            pl.BlockSpec((tm, d), lambda i, f: (i, 0)),
            pl.BlockSpec((1, d), lambda i, f: (0, 0)),
            pl.BlockSpec((None, d, tf), lambda i, f: (0, 0, f)),
            pl.BlockSpec((None, d, tf), lambda i, f: (0, 0, f)),
            pl.BlockSpec((None, tf, d), lambda i, f: (0, f, 0)),
        ],
        out_specs=pl.BlockSpec((tm, d), lambda i, f: (i, 0)),
        out_shape=jax.ShapeDtypeStruct((m, d), F32),
        scratch_shapes=[pltpu.VMEM((tm, d), BF16)],
        compiler_params=pltpu.CompilerParams(dimension_semantics=("parallel", "arbitrary"),
                                             vmem_limit_bytes=FFN_VMEM_LIMIT),
        name="ffn",
    )(x, ln, wg, wu, wd)


def _inproj_body(x_ref, ln_ref, w_ref, cos_ref, sin_ref, z_ref, h_ref, zb0, zb1, *,
                 rope_tiles, col_tiles):
    j = pl.program_id(1)
    bufs = (zb0, zb1)
    assert 0 < rope_tiles < col_tiles

    def finish_rope(src):
        for r0 in range(0, z_ref.shape[0], ROPE_ROW_BAND):
            rows = slice(r0, r0 + ROPE_ROW_BAND)
            cos = cos_ref[rows, :]
            sin = sin_ref[rows, :]
            for c in range(z_ref.shape[1] // HEAD_DIM):
                cols = slice(c * HEAD_DIM, (c + 1) * HEAD_DIM)
                zc = src[rows, cols]
                z_ref[rows, cols] = zc * cos + pltpu.roll(zc, HEAD_DIM // 2, axis=1) * sin

    @pl.when(j == 0)
    def _():
        h_ref[...] = _rmsnorm(x_ref[...], ln_ref[...]).astype(BF16)
        zb0[...] = _dot(h_ref[...], w_ref[...])

    for parity in (0, 1):
        cur, prev = bufs[parity], bufs[1 - parity]
        mine = (j % 2) == parity

        @pl.when(mine & (j >= 1) & (j < rope_tiles))
        def _(cur=cur, prev=prev):
            cur[...] = _dot(h_ref[...], w_ref[...])
            finish_rope(prev)

    @pl.when(j == rope_tiles)
    def _():
        finish_rope(bufs[(rope_tiles - 1) % 2])

    @pl.when(j > rope_tiles)
    def _():
        z_ref[...] = _dot(h_ref[...], w_ref[...])


def _inproj(x, ln, w, cos, sin, *, tm, tn):
    m, d = x.shape
    tm = min(tm, m)
    n = w.shape[1]
    col_tiles = n // tn
    rope_tiles = 2 * QKV_WIDTH // tn
    pos_tiles = cos.shape[0] // tm
    return pl.pallas_call(
        functools.partial(_inproj_body, rope_tiles=rope_tiles, col_tiles=col_tiles),
        grid=(m // tm, col_tiles + 1),
        in_specs=[
            pl.BlockSpec((tm, d), lambda i, j: (i, 0)),
            pl.BlockSpec((1, d), lambda i, j: (0, 0)),
            pl.BlockSpec((d, tn), lambda i, j: (0, jnp.where(j < rope_tiles, j, j - 1))),
            pl.BlockSpec((tm, HEAD_DIM), lambda i, j: (i % pos_tiles, 0)),
            pl.BlockSpec((tm, HEAD_DIM), lambda i, j: (i % pos_tiles, 0)),
        ],
        out_specs=pl.BlockSpec((tm, tn), lambda i, j: (i, jnp.maximum(j - 1, 0))),
        out_shape=jax.ShapeDtypeStruct((m, n), F32),
        scratch_shapes=[pltpu.VMEM((tm, d), BF16), pltpu.VMEM((tm, tn), F32),
                        pltpu.VMEM((tm, tn), F32)],
        compiler_params=_params("parallel", "arbitrary"),
        name="inproj",
    )(x, ln, w, cos, sin)


def _rope_tables(pos):
    half = HEAD_DIM // 2
    inv_freq = jnp.exp(-math.log(ROPE_THETA) * jnp.arange(half, dtype=F32) / half)
    ang = pos.astype(F32)[:, None] * inv_freq[None, :]
    cos, sin = jnp.cos(ang), jnp.sin(ang)
    return jnp.concatenate([cos, cos], axis=-1), jnp.concatenate([-sin, sin], axis=-1)


def _attn_prompt_body(q1, q2, q3, k1, k2, k3, v1, v2, v3,
                      attn_ref, kv128_ref, kv512_ref, kv2048_ref,
                      o_s, m_s, l_s):
    seq = q1.shape[0]
    blk = BAND_BLOCK
    for kv_ref, k_ref, v_ref in ((kv128_ref, k1, v1), (kv512_ref, k2, v2), (kv2048_ref, k3, v3)):
        keep = kv_ref.shape[1]
        kv_ref[0] = k_ref[pl.ds(seq - keep, keep), :]
        kv_ref[1] = v_ref[pl.ds(seq - keep, keep), :]

    row = lax.broadcasted_iota(jnp.int32, (blk, blk), 0)
    col = lax.broadcasted_iota(jnp.int32, (blk, blk), 1)
    own_mask = col <= row
    prev_mask = col >= row
    ones = jnp.ones((blk, HEAD_DIM), BF16)

    groups = ((q1, k1, v1), (q2, k2, v2), (q3, k3, v3))
    for g, ((q_ref, k_ref, v_ref), (_, dil)) in enumerate(zip(groups, ATTN_GROUPS)):
        nblk = seq // dil // blk
        has_prev = nblk > 1

        def rows(ref, start, dil=dil):
            if dil == 1:
                return ref[pl.ds(start, blk), :]
            return ref[pl.ds(start, blk, stride=dil), :]

        def sub(it, carry, g=g, dil=dil, nblk=nblk, has_prev=has_prev,
                q_ref=q_ref, k_ref=k_ref, v_ref=v_ref, rows=rows):
            work = []
            for u in range(PROMPT_BLOCKS_PER_TRIP):
                p = it * PROMPT_BLOCKS_PER_TRIP + u
                w = {"r": p // nblk, "n": p % nblk}
                w["qs"] = w["r"] + dil * (w["n"] * blk)
                w["q"] = rows(q_ref, w["qs"]).astype(BF16)
                w["kc"] = rows(k_ref, w["qs"]).astype(BF16)
                if has_prev:
                    w["ps"] = w["r"] + dil * (jnp.maximum(w["n"] - 1, 0) * blk)
                    w["kp"] = rows(k_ref, w["ps"]).astype(BF16)
                work.append(w)
            for w in work:
                w["s_own"] = jnp.where(own_mask, _dot_nt(w["q"], w["kc"]) * ATTN_SCALE, NEG_INF)
                if has_prev:
                    pm = jnp.logical_and(prev_mask, w["n"] > 0)
                    w["s_prev"] = jnp.where(pm, _dot_nt(w["q"], w["kp"]) * ATTN_SCALE, NEG_INF)
            for w in work:
                m = jnp.max(w["s_own"], axis=-1, keepdims=True)
                if has_prev:
                    m = jnp.maximum(m, jnp.max(w["s_prev"], axis=-1, keepdims=True))
                w["m"] = m
                w["e_own"] = jnp.exp(w["s_own"] - m).astype(BF16)
                if has_prev:
                    w["e_prev"] = jnp.exp(w["s_prev"] - m).astype(BF16)
            for w in work:
                acc = _dot(w["e_own"], rows(v_ref, w["qs"]).astype(BF16))
                l = _dot(w["e_own"], ones)
                if has_prev:
                    acc = acc + _dot(w["e_prev"], rows(v_ref, w["ps"]).astype(BF16))
                    l = l + _dot(w["e_prev"], ones)
                w["acc"] = acc
                w["l"] = l
            for w in work:
                if dil == 1:
                    idx = pl.ds(w["qs"], blk)
                else:
                    idx = pl.ds(w["qs"], blk, stride=dil)
                o_s[g, idx, :] = w["acc"]
                m_s[g, idx, :] = jnp.broadcast_to(w["m"], (blk, HEAD_DIM))
                l_s[g, idx, :] = w["l"]
            return carry

        lax.fori_loop(0, dil * nblk // PROMPT_BLOCKS_PER_TRIP, sub, 0)

    def combine(t, carry):
        idx = pl.ds(pl.multiple_of(t * blk, blk), blk)
        ms = [m_s[g, idx, :] for g in range(N_GROUPS)]
        mx = jnp.maximum(jnp.maximum(ms[0], ms[1]), ms[2])
        num = jnp.zeros((blk, HEAD_DIM), F32)
        den = jnp.zeros((blk, HEAD_DIM), F32)
        for g in range(N_GROUPS):
            w = jnp.exp(ms[g] - mx)
            num = num + w * o_s[g, idx, :]
            den = den + w * l_s[g, idx, :]
        attn_ref[idx, :] = (num / den).astype(attn_ref.dtype)
        return carry

    lax.fori_loop(0, seq // blk, combine, 0)


def _attn_prompt(z3, batch, seq):
    hd = HEAD_DIM
    nh = HEADS_PER_GROUP

    def col_spec(base):
        return pl.BlockSpec((None, seq, hd), lambda b, h, base=base: (b, 0, base + h))

    in_specs = [col_spec((t * N_GROUPS + g) * nh) for t in range(3) for g in range(N_GROUPS)]
    keeps = [min(w, seq) for w, _ in ATTN_GROUPS]
    kv_shapes = [jax.ShapeDtypeStruct((1, batch, 2, nh, keep, hd), F32) for keep in keeps]
    kv_specs = [pl.BlockSpec((None, None, 2, None, keep, hd), lambda b, h: (0, b, 0, h, 0, 0))
                for keep in keeps]
    return pl.pallas_call(
        _attn_prompt_body,
        grid=(batch, nh),
        in_specs=in_specs,
        out_specs=[pl.BlockSpec((None, seq, hd), lambda b, h: (b, 0, h))] + kv_specs,
        out_shape=[jax.ShapeDtypeStruct((batch, seq, nh * hd), BF16)] + kv_shapes,
        scratch_shapes=[pltpu.VMEM((N_GROUPS, seq, hd), F32) for _ in range(3)],
        compiler_params=_params("parallel", "parallel"),
        name="attn_prompt",
    )(*([z3] * 9))


def _attn_sample_fetch(c3_hbm, c3_buf, c3_sem, *, nq, seq_id, num_seqs):
    hd = HEAD_DIM
    pad = SUBLANES - nq
    b = seq_id
    slot = b % 2

    def c3_copy(seq, s):
        return pltpu.make_async_copy(c3_hbm.at[0, seq, :, :, :, pl.ds(0, nq), :],
                                     c3_buf.at[s, :, :, :, pl.ds(0, nq), :], c3_sem.at[s])

    @pl.when(b == 0)
    def _():
        c3_buf[:, :, :, :, pl.ds(nq, pad), :] = jnp.zeros(c3_buf.shape[:4] + (pad, hd), F32)
        c3_copy(0, 0).start()

    @pl.when(b + 1 < num_seqs)
    def _():
        c3_copy(b + 1, 1 - slot).start()

    c3_copy(b, slot).wait()
    return c3_buf.at[slot]


def _attn_sample_compute(q_ref, k_ref, v_ref, caches, attn_ref):
    nq = q_ref.shape[2]
    hd = HEAD_DIM
    pad = SUBLANES - nq

    def cache_rows(g, kv, h):
        x = caches[g][kv, h]
        return x.reshape(-1, hd).astype(BF16)

    def mask_for(g):
        n = caches[g].shape[2] * (caches[g].shape[3] if g == 2 else 1)
        row = lax.broadcasted_iota(jnp.int32, (SUBLANES, n), 0)
        col = lax.broadcasted_iota(jnp.int32, (SUBLANES, n), 1)
        if g == 0:
            return col >= row
        step = ATTN_GROUPS[1][1] if g == 1 else caches[g].shape[3]
        assert step & (step - 1) == 0
        return jnp.bitwise_and(col, step - 1) == row

    masks = [mask_for(g) for g in range(N_GROUPS)]
    qrow = lax.broadcasted_iota(jnp.int32, (SUBLANES, 1), 0)

    def scores(h):
        parts = []
        news = []
        for g in range(N_GROUPS):
            q = jnp.concatenate([q_ref[g, h], jnp.zeros((pad, hd), F32)], axis=0).astype(BF16)
            s = _dot_nt(q, cache_rows(g, 0, h)) * ATTN_SCALE
            parts.append((jnp.where(masks[g], s, NEG_INF), g))
            qf = q.astype(F32)
            kn = k_ref[g, h].astype(BF16).astype(F32)
            vn = v_ref[g, h].astype(BF16).astype(F32)
            if g == 0:
                for c in range(nq):
                    sn = jnp.sum(qf * kn[c:c + 1, :], axis=-1, keepdims=True) * ATTN_SCALE
                    news.append((jnp.where(qrow >= c, sn, NEG_INF), vn[c:c + 1, :]))
            else:
                knp = jnp.concatenate([kn, jnp.zeros((pad, hd), F32)], axis=0)
                sn = jnp.sum(qf * knp, axis=-1, keepdims=True) * ATTN_SCALE
                for c in range(nq):
                    news.append((jnp.where(qrow == c, sn, NEG_INF), vn[c:c + 1, :]))
        return parts, news

    def softmax(parts, news):
        m = parts[0][0].max(axis=-1, keepdims=True)
        for s, _ in parts[1:]:
            m = jnp.maximum(m, s.max(axis=-1, keepdims=True))
        for s, _ in news:
            m = jnp.maximum(m, s)
        es = [jnp.exp(s - m) for s, _ in parts]
        l = jnp.zeros((SUBLANES, 1), F32)
        for e in es:
            l = l + jnp.sum(e, axis=-1, keepdims=True)
        acc = jnp.zeros((SUBLANES, hd), F32)
        for s, v in news:
            e = jnp.exp(s - m)
            l = l + e
            acc = acc + e.astype(BF16).astype(F32) * v
        return es, l, acc

    def trip(it, carry):
        heads = [it * SAMPLE_HEADS_PER_TRIP + u for u in range(SAMPLE_HEADS_PER_TRIP)]
        staged = [scores(h) for h in heads]
        soft = [softmax(parts, news) for parts, news in staged]
        outs = []
        for h, (parts, _), (es, l, acc) in zip(heads, staged, soft):
            for e, (_, g) in zip(es, parts):
                acc = acc + _dot(e.astype(BF16), cache_rows(g, 1, h))
            outs.append((acc / l)[:nq, :])
        for h, out in zip(heads, outs):
            attn_ref[h] = out
        return carry

    lax.fori_loop(0, HEADS_PER_GROUP // SAMPLE_HEADS_PER_TRIP, trip, 0)


def _attn_sample_specs(q_s, c128, c512, c2048, seq_of):
    nb, _, nh, nq, hd = q_s.shape
    new_spec = pl.BlockSpec((None, N_GROUPS, nh, nq, hd), lambda *g: (seq_of(*g), 0, 0, 0, 0))
    for cache, (window, _) in zip((c128, c512, c2048), ATTN_GROUPS):
        assert cache.shape[4] == window, "the caches must hold a full window"
    assert nq <= ATTN_GROUPS[1][1]
    dil3 = ATTN_GROUPS[2][1]
    rows3 = c2048.shape[4] // dil3
    c2048 = c2048.reshape(1, nb, 2, nh, rows3, dil3, hd)
    in_specs = [
        new_spec, new_spec, new_spec,
        pl.BlockSpec((None, None, 2, nh, c128.shape[4], hd), lambda *g: (0, seq_of(*g), 0, 0, 0, 0)),
        pl.BlockSpec((None, None, 2, nh, c512.shape[4], hd), lambda *g: (0, seq_of(*g), 0, 0, 0, 0)),
        pl.BlockSpec(memory_space=pl.ANY),
    ]
    out_spec = pl.BlockSpec((None, nh, nq, hd), lambda *g: (seq_of(*g), 0, 0, 0))
    out_shape = jax.ShapeDtypeStruct((nb, nh, nq, hd), F32)
    scratch = [pltpu.VMEM((2, 2, nh, rows3, SUBLANES, hd), F32), pltpu.SemaphoreType.DMA((2,))]
    return c2048, in_specs, out_spec, out_shape, scratch


def _rglru_body(*refs, chained, before_main=None, after_main=None):
    if chained:
        (x_ref, y_ref, cw_ref, cb_ref, wri_ref, br_ref, bi_ref, lam_ref,
         rec_ref, hout_ref, sa, sb, sh, tail_ref, carry_ref) = refs
    else:
        (x_ref, y_ref, h0_ref, cw_ref, cb_ref, wri_ref, br_ref, bi_ref, lam_ref,
         rec_ref, hout_ref, sa, sb, sh) = refs
    tc, ch = x_ref.shape
    tiles = (tc // SUBLANES, SUBLANES, ch)
    step = pl.program_id(2) if chained else None
    row8 = lax.broadcasted_iota(jnp.int32, tiles, 1)

    if chained:
        @pl.when(step == 0)
        def _():
            tail_ref[...] = jnp.zeros_like(tail_ref)
            carry_ref[...] = jnp.zeros_like(carry_ref)

    if before_main is not None:
        before_main()
    cur = x_ref[...]
    if chained:
        prev = jnp.concatenate([tail_ref[...], cur[:tc - SUBLANES, :]], axis=0)
        tail_ref[...] = cur[tc - SUBLANES:, :]
    else:
        prev = cur
    cur = cur.reshape(tiles)
    prev = prev.reshape(tiles)

    xc = cb_ref[...]
    for j in range(CONV_W):
        d = CONV_W - 1 - j
        xs = cur if d == 0 else pltpu.roll(jnp.where(row8 >= SUBLANES - d, prev, cur), d, axis=1)
        xc = xc + xs * cw_ref[j:j + 1, :]

    xc2 = xc.reshape(tc, ch)
    pre_r, pre_i = [], []
    for n in range(ch // LRU_BLOCK_W):
        blk = xc2[:, n * LRU_BLOCK_W:(n + 1) * LRU_BLOCK_W].astype(BF16)
        ri = _dot(blk, wri_ref[n])
        pre_r.append(ri[:, :LRU_BLOCK_W])
        pre_i.append(ri[:, LRU_BLOCK_W:])
    r = jax.nn.sigmoid(jnp.concatenate(pre_r, axis=-1) + br_ref[...])
    i = jax.nn.sigmoid(jnp.concatenate(pre_i, axis=-1) + bi_ref[...])
    log_a = r * (-LRU_C * jax.nn.softplus(-lam_ref[...]))
    a = jnp.exp(log_a)
    th = jnp.tanh(log_a)
    p = -2.0 * th
    w = 1.0 - th
    mult = jnp.where(p > 0.0, p * lax.rsqrt(p * w), 0.0)
    b = mult * (i * xc2)

    if not chained:
        a = jnp.where(row8 < CONV_W, 1.0, a.reshape(tiles)).reshape(tc, ch)
        b = jnp.where(row8 < CONV_W, h0_ref[...].reshape(tiles), b.reshape(tiles)).reshape(tc, ch)

    slabs = ch // LANES
    for c in range(slabs):
        sa[c] = a[:, c * LANES:(c + 1) * LANES]
        sb[c] = b[:, c * LANES:(c + 1) * LANES]
    span = SCAN_TILES_PER_PASS
    tile_row = lax.broadcasted_iota(jnp.int32, (span, LANES), 0)
    for c in range(slabs):
        carry = carry_ref[c, 0:1, :] if chained else None
        for t0 in range(0, tc, span * SUBLANES):
            rows = [pl.ds(t0 + j, span, stride=SUBLANES) for j in range(SUBLANES)]
            a_rows = [sa[c, idx, :] for idx in rows]
            b_rows = [sb[c, idx, :] for idx in rows]
            acc_a, acc_b = [a_rows[0]], [b_rows[0]]
            for j in range(1, SUBLANES):
                acc_b.append(a_rows[j] * acc_b[-1] + b_rows[j])
                acc_a.append(a_rows[j] * acc_a[-1])
            if chained:
                ta, tb = acc_a[-1], acc_b[-1]
                shift = 1
                while shift < span:
                    valid = tile_row >= shift
                    tb = jnp.where(valid, ta * pltpu.roll(tb, shift, axis=0) + tb, tb)
                    ta = jnp.where(valid, ta * pltpu.roll(ta, shift, axis=0), ta)
                    shift *= 2
                state_out = ta * carry + tb
                state_in = jnp.where(tile_row == 0, carry, pltpu.roll(state_out, 1, axis=0))
                acc_b = [pa * state_in + pb for pa, pb in zip(acc_a, acc_b)]
                carry = state_out[span - 1:span, :]
            for j, idx in enumerate(rows):
                sh[c, idx, :] = acc_b[j]
        if chained:
            carry_ref[c] = jnp.broadcast_to(carry, (SUBLANES, LANES))
    h = jnp.concatenate([sh[c] for c in range(slabs)], axis=-1)

    rec_ref[...] = (h * jax.nn.gelu(y_ref[...])).astype(rec_ref.dtype)
    if after_main is not None:
        after_main()
    if chained:
        @pl.when(step == pl.num_programs(2) - 1)
        def _():
            hout_ref[...] = h[tc - 1:tc, :]
    else:
        hout_ref[...] = h


def _rglru_params_specs(ch, idx):
    nblk = ch // LRU_BLOCK_W
    return [
        pl.BlockSpec((CONV_W, ch), lambda *g: (0, idx(*g))),
        pl.BlockSpec((1, ch), lambda *g: (0, idx(*g))),
        pl.BlockSpec((nblk, LRU_BLOCK_W, 2 * LRU_BLOCK_W), lambda *g: (idx(*g), 0, 0)),
        pl.BlockSpec((1, ch), lambda *g: (0, idx(*g))),
        pl.BlockSpec((1, ch), lambda *g: (0, idx(*g))),
        pl.BlockSpec((1, ch), lambda *g: (0, idx(*g))),
    ]


N_RGLRU_IN = 8
N_RGLRU_SCRATCH = 5


def _rglru_prompt_attn_sample_body(*refs):
    rg_in, rest = refs[:N_RGLRU_IN], refs[N_RGLRU_IN:]
    at_in, rest = rest[:6], rest[6:]
    (rec_ref, hout_ref, attn_ref), rest = rest[:3], rest[3:]
    rg_scratch, at_scratch = rest[:N_RGLRU_SCRATCH], rest[N_RGLRU_SCRATCH:]
    q_ref, k_ref, v_ref, c1_ref, c2_ref, c3_hbm = at_in
    n1, n2 = pl.num_programs(1), pl.num_programs(2)
    seq_id = (pl.program_id(0) * n1 + pl.program_id(1)) * n2 + pl.program_id(2)
    fetched = []

    def fetch():
        fetched.append(_attn_sample_fetch(c3_hbm, *at_scratch, nq=q_ref.shape[2], seq_id=seq_id,
                                          num_seqs=pl.num_programs(0) * n1 * n2))

    def attend():
        _attn_sample_compute(q_ref, k_ref, v_ref, (c1_ref, c2_ref, fetched[0]), attn_ref)

    _rglru_body(*rg_in, rec_ref, hout_ref, *rg_scratch, chained=True,
                before_main=fetch, after_main=attend)


def _rglru_prompt_attn_sample(z3, x_col, y_col, params, q_s, k_s, v_s, c128, c512, c2048,
                              *, d_rnn, tc, ch):
    batch, seq, _ = z3.shape
    nc = d_rnn // ch
    nt = seq // tc
    assert q_s.shape[0] == batch * nc * nt, "one sample sequence per recurrence step"
    c2048, at_specs, at_out_spec, at_out_shape, at_scratch = _attn_sample_specs(
        q_s, c128, c512, c2048, lambda b, c, t: (b * nc + c) * nt + t)
    rec, hlast, attn = pl.pallas_call(
        _rglru_prompt_attn_sample_body,
        grid=(batch, nc, nt),
        in_specs=[
            pl.BlockSpec((None, tc, ch), lambda b, c, t: (b, t, x_col // ch + c)),
            pl.BlockSpec((None, tc, ch), lambda b, c, t: (b, t, y_col // ch + c)),
        ] + _rglru_params_specs(ch, lambda b, c, t: c) + at_specs,
        out_specs=[
            pl.BlockSpec((None, tc, ch), lambda b, c, t: (b, t, c)),
            pl.BlockSpec((None, 1, ch), lambda b, c, t: (b, 0, c)),
            at_out_spec,
        ],
        out_shape=[jax.ShapeDtypeStruct((batch, seq, d_rnn), BF16),
                   jax.ShapeDtypeStruct((batch, 1, d_rnn), F32),
                   at_out_shape],
        scratch_shapes=[pltpu.VMEM((ch // LANES, tc, LANES), F32) for _ in range(3)]
        + [pltpu.VMEM((SUBLANES, ch), F32), pltpu.VMEM((ch // LANES, SUBLANES, LANES), F32)]
        + at_scratch,
        compiler_params=_params("arbitrary", "arbitrary", "arbitrary"),
        name="rglru_prompt_attn_sample",
    )(z3, z3, *params, q_s, k_s, v_s, c128, c512, c2048)
    return rec, hlast, attn


def _rglru_sample(x8, y8, h8, params, *, tc, ch):
    rows, d_rnn = x8.shape
    nc = d_rnn // ch
    row_spec = pl.BlockSpec((tc, ch), lambda c, t: (t, c))
    return pl.pallas_call(
        functools.partial(_rglru_body, chained=False),
        grid=(nc, rows // tc),
        in_specs=[row_spec, row_spec, row_spec] + _rglru_params_specs(ch, lambda c, t: c),
        out_specs=[row_spec, row_spec],
        out_shape=[jax.ShapeDtypeStruct((rows, d_rnn), F32),
                   jax.ShapeDtypeStruct((rows, d_rnn), F32)],
        scratch_shapes=[pltpu.VMEM((ch // LANES, tc, LANES), F32) for _ in range(3)],
        compiler_params=_params("parallel", "parallel"),
        name="rglru_sample",
    )(x8, y8, h8, *params)


def _merge_body(x_ref, attn_ref, rec_ref, ga0_ref, ga1_ref, gr0_ref, gr1_ref,
                wba_ref, wbr_ref, wout_ref, o_ref):
    pa = _dot(attn_ref[...].astype(BF16), wba_ref[...])
    pr = _dot(rec_ref[...].astype(BF16), wbr_ref[...])
    ga = jnp.concatenate([ga0_ref[...], ga1_ref[...]], axis=-1)
    gr = jnp.concatenate([gr0_ref[...], gr1_ref[...]], axis=-1)
    merged = jax.nn.sigmoid(ga) * pa + jax.nn.sigmoid(gr) * pr
    o_ref[...] = x_ref[...] + _dot(merged.astype(BF16), wout_ref[...])


def _merge(x, attn, rec, z, g_col, wba, wbr, wout, *, tm):
    m, d = x.shape
    tm = min(tm, m)
    half = d // 2
    gb = g_col // half

    def whole(w):
        return pl.BlockSpec(w.shape, lambda i: (0, 0), pipeline_mode=pl.Buffered(1))

    def gate_spec(k):
        return pl.BlockSpec((tm, half), lambda i, k=k: (i, gb + k))

    return pl.pallas_call(
        _merge_body,
        grid=(m // tm,),
        in_specs=[
            pl.BlockSpec((tm, d), lambda i: (i, 0)),
            pl.BlockSpec((tm, attn.shape[1]), lambda i: (i, 0)),
            pl.BlockSpec((tm, rec.shape[1]), lambda i: (i, 0)),
            gate_spec(0), gate_spec(1), gate_spec(2), gate_spec(3),
            whole(wba), whole(wbr), whole(wout),
        ],
        out_specs=pl.BlockSpec((tm, d), lambda i: (i, 0)),
        out_shape=jax.ShapeDtypeStruct((m, d), F32),
        compiler_params=_params("parallel"),
        name="merge",
    )(x, attn, rec, z, z, z, z, wba, wbr, wout)


def _ple_body(x_ref, pe_ref, lnp_ref, wpg_ref, wpp_ref, lnf_ref, y_ref):
    for r0 in range(0, x_ref.shape[0], ROW_BAND):
        rows = slice(r0, r0 + ROW_BAND)
        x = x_ref[rows, :]
        hn = _rmsnorm(x, lnp_ref[...]).astype(BF16)
        gate = jax.nn.sigmoid(_dot(hn, wpg_ref[...]))
        proj = _dot(pe_ref[rows, :].astype(BF16), wpp_ref[...])
        y_ref[rows, :] = _rmsnorm(x + gate * proj, lnf_ref[...])


def _ple(x, pe, lnp, wpg, wpp, lnf, *, tm):
    m, d = x.shape
    tm = min(tm, m)

    def whole(w):
        return pl.BlockSpec(w.shape, lambda i: (0, 0), pipeline_mode=pl.Buffered(1))

    return pl.pallas_call(
        _ple_body,
        grid=(m // tm,),
        in_specs=[
            pl.BlockSpec((tm, d), lambda i: (i, 0)),
            pl.BlockSpec((tm, pe.shape[1]), lambda i: (i, 0)),
            pl.BlockSpec((1, d), lambda i: (0, 0)),
            whole(wpg), whole(wpp),
            pl.BlockSpec((1, d), lambda i: (0, 0)),
        ],
        out_specs=pl.BlockSpec((tm, d), lambda i: (i, 0)),
        out_shape=jax.ShapeDtypeStruct((m, d), F32),
        compiler_params=_params("parallel"),
        name="ple",
    )(x, pe, lnp, wpg, wpp, lnf)


def kernel(x_prompt, x_sample, cache_kv_w128, cache_kv_w512, cache_kv_w2048, state_conv, state_rglru, p_prompt, p_sample, ln_ffn1, w_ffn1_gate, w_ffn1_up, w_ffn1_down, ln_mix, w_in, conv_w, conv_b, w_rgate, b_rgate, w_igate, b_igate, lru_lambda, w_branch_attn, w_branch_rec, w_out, ln_ffn2, w_ffn2_gate, w_ffn2_up, w_ffn2_down, ln_ple, w_ple_gate, w_ple_proj, ln_final):
    batch, seq, d = x_prompt.shape
    nb, nq, _ = x_sample.shape
    past_len = cache_kv_w2048.shape[4]
    d_rnn = conv_w.shape[-1]
    depth = w_in.shape[0]
    assert depth == 1

    xp = x_prompt.reshape(batch * seq, d)
    xs = x_sample.reshape(nb * nq, d)
    cos_p, sin_p = _rope_tables(jnp.arange(seq))
    cos_s, sin_s = _rope_tables(past_len + (jnp.arange(nb * nq) % nq))

    l = 0
    row = lambda v: v[l].reshape(1, -1)
    wri = jnp.concatenate([w_rgate[l], w_igate[l]], axis=-1).astype(BF16)
    rg_params = (conv_w[l], row(conv_b), wri, row(b_rgate), row(b_igate), row(lru_lambda))
    w1 = (w_ffn1_gate, w_ffn1_up, w_ffn1_down)
    w2 = (w_ffn2_gate, w_ffn2_up, w_ffn2_down)
    ffn_tiles = dict(tm=1024, tf=256)
    ffn_tiles_sample = dict(tm=512, tf=512)
    win = w_in[l].astype(BF16)
    wba = w_branch_attn[l].astype(BF16)
    wbr = w_branch_rec[l].astype(BF16)
    wout = w_out[l].astype(BF16)
    wpg = w_ple_gate[l].astype(BF16)
    wpp = w_ple_proj[l].astype(BF16)
    lnf = ln_final.reshape(1, -1)

    x_col = 3 * QKV_WIDTH
    y_col = x_col + d_rnn
    g_col = y_col + d_rnn
    tm = 512

    xs = _ffn(xs, row(ln_ffn1), *w1, **ffn_tiles_sample)
    zs = _inproj(xs, row(ln_mix), win, cos_s, sin_s, tm=tm, tn=512)
    qkv = zs[:, :3 * QKV_WIDTH].reshape(nb, nq, 3, N_GROUPS, HEADS_PER_GROUP, HEAD_DIM)
    qkv = qkv.transpose(2, 0, 3, 4, 1, 5)

    xp = _ffn(xp, row(ln_ffn1), *w1, **ffn_tiles)
    zp = _inproj(xp, row(ln_mix), win, cos_p, sin_p, tm=1024, tn=1024)
    zp3 = zp.reshape(batch, seq, -1)
    attn_p, kv1_p, kv2_p, kv3_p = _attn_prompt(zp3, batch, seq)
    rec_p, hlast_p, attn_s = _rglru_prompt_attn_sample(
        zp3, x_col, y_col, rg_params, qkv[0], qkv[1], qkv[2],
        cache_kv_w128, cache_kv_w512, cache_kv_w2048, d_rnn=d_rnn, tc=256, ch=1024)
    conv_p = zp3[:, seq - (CONV_W - 1):, x_col:x_col + d_rnn][None]
    xp = _merge(xp, attn_p.reshape(batch * seq, -1), rec_p.reshape(batch * seq, -1), zp, g_col,
                wba, wbr, wout, tm=256)
    xp = _ffn(xp, row(ln_ffn2), *w2, **ffn_tiles)
    yp = _ple(xp, p_prompt[l].reshape(batch * seq, -1), row(ln_ple), wpg, wpp, lnf, tm=512)

    attn_s = attn_s.transpose(0, 2, 1, 3).reshape(nb * nq, -1)
    kv_s = [jnp.stack([qkv[1][:, g], qkv[2][:, g]], axis=1)[None] for g in range(N_GROUPS)]

    x_rec = zs[:, x_col:x_col + d_rnn].reshape(nb, nq, d_rnn)
    y_rec = zs[:, y_col:y_col + d_rnn].reshape(nb, nq, d_rnn)
    zero1 = jnp.zeros((nb, 1, d_rnn), F32)
    x8 = jnp.concatenate([zero1, state_conv[l], x_rec], axis=1).reshape(nb * SUBLANES, d_rnn)
    y8 = jnp.concatenate([jnp.zeros((nb, CONV_W, d_rnn), F32), y_rec], axis=1).reshape(nb * SUBLANES, d_rnn)
    h8 = jnp.concatenate([jnp.zeros((nb, CONV_W - 1, d_rnn), F32), state_rglru[l][:, None],
                          jnp.zeros((nb, nq, d_rnn), F32)], axis=1).reshape(nb * SUBLANES, d_rnn)
    rec8, hs8 = _rglru_sample(x8, y8, h8, rg_params, tc=256, ch=1024)
    rec_s = rec8.reshape(nb, SUBLANES, d_rnn)[:, CONV_W:].reshape(nb * nq, d_rnn)
    hlast_s = hs8.reshape(nb, SUBLANES, d_rnn)[:, SUBLANES - 1][None]
    conv_s = jnp.concatenate([state_conv[l], x_rec], axis=1)[:, -(CONV_W - 1):][None]
    xs = _merge(xs, attn_s, rec_s, zs, g_col, wba, wbr, wout, tm=256)
    xs = _ffn(xs, row(ln_ffn2), *w2, **ffn_tiles_sample)
    ys = _ple(xs, p_sample[l].reshape(nb * nq, -1), row(ln_ple), wpg, wpp, lnf, tm=512)

    return (yp.reshape(batch, seq, d), ys.reshape(nb, nq, d),
            kv1_p, kv2_p, kv3_p, conv_p, hlast_p.reshape(1, batch, d_rnn),
            kv_s[0], kv_s[1], kv_s[2], conv_s, hlast_s)
```

```python
import functools
import math

import jax
import jax.numpy as jnp
from jax import lax
from jax.experimental import pallas as pl
from jax.experimental.pallas import tpu as pltpu

F32 = jnp.float32
BF16 = jnp.bfloat16

HEAD_DIM = 128
HEADS_PER_GROUP = 8
ATTN_GROUPS = ((128, 1), (512, 4), (2048, 16))
N_GROUPS = 3
QKV_WIDTH = N_GROUPS * HEADS_PER_GROUP * HEAD_DIM
ATTN_SCALE = HEAD_DIM ** -0.5
BAND_BLOCK = 128
ROPE_THETA = 10000.0
LRU_BLOCK_W = 128
LRU_C = 8.0
CONV_W = 4
EPS = 1e-6

SUBLANES = 8
LANES = 128
VMEM_LIMIT = 48 * 1024 * 1024
FFN_VMEM_LIMIT = 56 * 1024 * 1024
NEG_INF = float("-inf")
PROMPT_BLOCKS_PER_TRIP = 16
SAMPLE_HEADS_PER_TRIP = 4
ROPE_ROW_BAND = 64
ROW_BAND = 256
SCAN_TILES_PER_PASS = 16


def _params(*semantics):
    return pltpu.CompilerParams(dimension_semantics=semantics, vmem_limit_bytes=VMEM_LIMIT)


def _rmsnorm(x, g):
    return x * lax.rsqrt(jnp.mean(x * x, axis=-1, keepdims=True) + EPS) * g


def _dot(a, b):
    return jnp.dot(a, b, preferred_element_type=F32)


def _dot_nt(a, b):
    return lax.dot_general(a, b, (((1,), (1,)), ((), ())), preferred_element_type=F32)


def _ffn_body(x_ref, ln_ref, wg_ref, wu_ref, wd_ref, o_ref, h_ref):
    f = pl.program_id(1)

    def half_swiglu(h):
        g = _dot(h, wg_ref[...].astype(BF16))
        u = _dot(h, wu_ref[...].astype(BF16))
        a = (0.5 * g * jax.nn.sigmoid(g)) * u
        return _dot(a.astype(BF16), wd_ref[...].astype(BF16))

    @pl.when(f == 0)
    def _():
        for r0 in range(0, x_ref.shape[0], ROW_BAND):
            rows = slice(r0, r0 + ROW_BAND)
            x = x_ref[rows, :]
            h = _rmsnorm(x, ln_ref[...]).astype(BF16)
            h_ref[rows, :] = h
            o_ref[rows, :] = x + half_swiglu(h)

    @pl.when(f > 0)
    def _():
        o_ref[...] += half_swiglu(h_ref[...])


def _ffn(x, ln, wg, wu, wd, *, tm, tf):
    m, d = x.shape
    tm = min(tm, m)
    ff = wg.shape[2]
    return pl.pallas_call(
        _ffn_body,
        grid=(m // tm, ff // tf),
        in_specs=[
            pl.BlockSpec((tm, d), lambda i, f: (i, 0)),
            pl.BlockSpec((1, d), lambda i, f: (0, 0)),
            pl.BlockSpec((None, d, tf), lambda i, f: (0, 0, f)),
            pl.BlockSpec((None, d, tf), lambda i, f: (0, 0, f)),
            pl.BlockSpec((None, tf, d), lambda i, f: (0, f, 0)),
        ],
        out_specs=pl.BlockSpec((tm, d), lambda i, f: (i, 0)),
        out_shape=jax.ShapeDtypeStruct((m, d), F32),
        scratch_shapes=[pltpu.VMEM((tm, d), BF16)],
        compiler_params=pltpu.CompilerParams(dimension_semantics=("parallel", "arbitrary"),
                                             vmem_limit_bytes=FFN_VMEM_LIMIT),
        name="ffn",
    )(x, ln, wg, wu, wd)


def _inproj_body(x_ref, ln_ref, w_ref, cos_ref, sin_ref, z_ref, h_ref, zb0, zb1, *,
                 rope_tiles, col_tiles):
    j = pl.program_id(1)
    bufs = (zb0, zb1)
    assert 0 < rope_tiles < col_tiles

    def finish_rope(src):
        for r0 in range(0, z_ref.shape[0], ROPE_ROW_BAND):
            rows = slice(r0, r0 + ROPE_ROW_BAND)
            cos = cos_ref[rows, :]
            sin = sin_ref[rows, :]
            for c in range(z_ref.shape[1] // HEAD_DIM):
                cols = slice(c * HEAD_DIM, (c + 1) * HEAD_DIM)
                zc = src[rows, cols]
                z_ref[rows, cols] = zc * cos + pltpu.roll(zc, HEAD_DIM // 2, axis=1) * sin

    @pl.when(j == 0)
    def _():
        h_ref[...] = _rmsnorm(x_ref[...], ln_ref[...]).astype(BF16)
        zb0[...] = _dot(h_ref[...], w_ref[...])

    for parity in (0, 1):
        cur, prev = bufs[parity], bufs[1 - parity]
        mine = (j % 2) == parity

        @pl.when(mine & (j >= 1) & (j < rope_tiles))
        def _(cur=cur, prev=prev):
            cur[...] = _dot(h_ref[...], w_ref[...])
            finish_rope(prev)

    @pl.when(j == rope_tiles)
    def _():
        finish_rope(bufs[(rope_tiles - 1) % 2])

    @pl.when(j > rope_tiles)
    def _():
        z_ref[...] = _dot(h_ref[...], w_ref[...])


def _inproj(x, ln, w, cos, sin, *, tm, tn):
    m, d = x.shape
    tm = min(tm, m)
    n = w.shape[1]
    col_tiles = n // tn
    rope_tiles = 2 * QKV_WIDTH // tn
    pos_tiles = cos.shape[0] // tm
    return pl.pallas_call(
        functools.partial(_inproj_body, rope_tiles=rope_tiles, col_tiles=col_tiles),
        grid=(m // tm, col_tiles + 1),
        in_specs=[
            pl.BlockSpec((tm, d), lambda i, j: (i, 0)),
            pl.BlockSpec((1, d), lambda i, j: (0, 0)),
            pl.BlockSpec((d, tn), lambda i, j: (0, jnp.where(j < rope_tiles, j, j - 1))),
            pl.BlockSpec((tm, HEAD_DIM), lambda i, j: (i % pos_tiles, 0)),
            pl.BlockSpec((tm, HEAD_DIM), lambda i, j: (i % pos_tiles, 0)),
        ],
        out_specs=pl.BlockSpec((tm, tn), lambda i, j: (i, jnp.maximum(j - 1, 0))),
        out_shape=jax.ShapeDtypeStruct((m, n), F32),
        scratch_shapes=[pltpu.VMEM((tm, d), BF16), pltpu.VMEM((tm, tn), F32),
                        pltpu.VMEM((tm, tn), F32)],
        compiler_params=_params("parallel", "arbitrary"),
        name="inproj",
    )(x, ln, w, cos, sin)


def _rope_tables(pos):
    half = HEAD_DIM // 2
    inv_freq = jnp.exp(-math.log(ROPE_THETA) * jnp.arange(half, dtype=F32) / half)
    ang = pos.astype(F32)[:, None] * inv_freq[None, :]
    cos, sin = jnp.cos(ang), jnp.sin(ang)
    return jnp.concatenate([cos, cos], axis=-1), jnp.concatenate([-sin, sin], axis=-1)


def _attn_prompt_body(q1, q2, q3, k1, k2, k3, v1, v2, v3,
                      attn_ref, kv128_ref, kv512_ref, kv2048_ref,
                      o_s, m_s, l_s):
    seq = q1.shape[0]
    blk = BAND_BLOCK
    for kv_ref, k_ref, v_ref in ((kv128_ref, k1, v1), (kv512_ref, k2, v2), (kv2048_ref, k3, v3)):
        keep = kv_ref.shape[1]
        kv_ref[0] = k_ref[pl.ds(seq - keep, keep), :]
        kv_ref[1] = v_ref[pl.ds(seq - keep, keep), :]

    row = lax.broadcasted_iota(jnp.int32, (blk, blk), 0)
    col = lax.broadcasted_iota(jnp.int32, (blk, blk), 1)
    own_mask = col <= row
    prev_mask = col >= row

    groups = ((q1, k1, v1), (q2, k2, v2), (q3, k3, v3))
    for g, ((q_ref, k_ref, v_ref), (_, dil)) in enumerate(zip(groups, ATTN_GROUPS)):
        nblk = seq // dil // blk
        has_prev = nblk > 1

        def rows(ref, start, dil=dil):
            if dil == 1:
                return ref[pl.ds(start, blk), :]
            return ref[pl.ds(start, blk, stride=dil), :]

        def sub(it, carry, g=g, dil=dil, nblk=nblk, has_prev=has_prev,
                q_ref=q_ref, k_ref=k_ref, v_ref=v_ref, rows=rows):
            work = []
            for u in range(PROMPT_BLOCKS_PER_TRIP):
                p = it * PROMPT_BLOCKS_PER_TRIP + u
                w = {"r": p // nblk, "n": p % nblk}
                w["qs"] = w["r"] + dil * (w["n"] * blk)
                w["q"] = rows(q_ref, w["qs"]).astype(BF16)
                w["kc"] = rows(k_ref, w["qs"]).astype(BF16)
                if has_prev:
                    w["ps"] = w["r"] + dil * (jnp.maximum(w["n"] - 1, 0) * blk)
                    w["kp"] = rows(k_ref, w["ps"]).astype(BF16)
                work.append(w)
            for w in work:
                if has_prev:
                    keys = jnp.concatenate([w["kp"], w["kc"]], axis=0)
                    mask = jnp.concatenate([jnp.logical_and(prev_mask, w["n"] > 0), own_mask], axis=1)
                else:
                    keys, mask = w["kc"], own_mask
                w["s"] = jnp.where(mask, _dot_nt(w["q"], keys) * ATTN_SCALE, NEG_INF)
            for w in work:
                w["m"] = jnp.max(w["s"], axis=-1, keepdims=True)
                w["e"] = jnp.exp(w["s"] - w["m"]).astype(BF16)
            for w in work:
                vals = rows(v_ref, w["qs"]).astype(BF16)
                if has_prev:
                    vals = jnp.concatenate([rows(v_ref, w["ps"]).astype(BF16), vals], axis=0)
                ones = jnp.ones((vals.shape[0], HEAD_DIM), BF16)
                both = _dot(w["e"], jnp.concatenate([vals, ones], axis=1))
                w["acc"] = both[:, :HEAD_DIM]
                w["l"] = both[:, HEAD_DIM:]
            for w in work:
                if dil == 1:
                    idx = pl.ds(w["qs"], blk)
                else:
                    idx = pl.ds(w["qs"], blk, stride=dil)
                o_s[g, idx, :] = w["acc"]
                m_s[g, idx, :] = jnp.broadcast_to(w["m"], (blk, HEAD_DIM))
                l_s[g, idx, :] = w["l"]
            return carry

        lax.fori_loop(0, dil * nblk // PROMPT_BLOCKS_PER_TRIP, sub, 0)

    def combine(t, carry):
        idx = pl.ds(pl.multiple_of(t * blk, blk), blk)
        ms = [m_s[g, idx, :] for g in range(N_GROUPS)]
        mx = jnp.maximum(jnp.maximum(ms[0], ms[1]), ms[2])
        num = jnp.zeros((blk, HEAD_DIM), F32)
        den = jnp.zeros((blk, HEAD_DIM), F32)
        for g in range(N_GROUPS):
            w = jnp.exp(ms[g] - mx)
            num = num + w * o_s[g, idx, :]
            den = den + w * l_s[g, idx, :]
        attn_ref[idx, :] = (num / den).astype(attn_ref.dtype)
        return carry

    lax.fori_loop(0, seq // blk, combine, 0)


def _attn_prompt(z3, batch, seq):
    hd = HEAD_DIM
    nh = HEADS_PER_GROUP

    def col_spec(base):
        return pl.BlockSpec((None, seq, hd), lambda b, h, base=base: (b, 0, base + h))

    in_specs = [col_spec((t * N_GROUPS + g) * nh) for t in range(3) for g in range(N_GROUPS)]
    keeps = [min(w, seq) for w, _ in ATTN_GROUPS]
    kv_shapes = [jax.ShapeDtypeStruct((1, batch, 2, nh, keep, hd), F32) for keep in keeps]
    kv_specs = [pl.BlockSpec((None, None, 2, None, keep, hd), lambda b, h: (0, b, 0, h, 0, 0))
                for keep in keeps]
    return pl.pallas_call(
        _attn_prompt_body,
        grid=(batch, nh),
        in_specs=in_specs,
        out_specs=[pl.BlockSpec((None, seq, hd), lambda b, h: (b, 0, h))] + kv_specs,
        out_shape=[jax.ShapeDtypeStruct((batch, seq, nh * hd), BF16)] + kv_shapes,
        scratch_shapes=[pltpu.VMEM((N_GROUPS, seq, hd), F32) for _ in range(3)],
        compiler_params=_params("parallel", "parallel"),
        name="attn_prompt",
    )(*([z3] * 9))


def _attn_sample_body(q_ref, k_ref, v_ref, c1_ref, c2_ref, c3_hbm, attn_ref, c3_buf, c3_sem, *,
                      seq_id, num_seqs):
    nq = q_ref.shape[2]
    hd = HEAD_DIM
    pad = SUBLANES - nq

    b = seq_id
    slot = b % 2

    def c3_copy(seq, s):
        return pltpu.make_async_copy(c3_hbm.at[0, seq, :, :, :, pl.ds(0, nq), :],
                                     c3_buf.at[s, :, :, :, pl.ds(0, nq), :], c3_sem.at[s])

    @pl.when(b == 0)
    def _():
        c3_buf[:, :, :, :, pl.ds(nq, pad), :] = jnp.zeros(c3_buf.shape[:4] + (pad, hd), F32)
        c3_copy(0, 0).start()

    @pl.when(b + 1 < num_seqs)
    def _():
        c3_copy(b + 1, 1 - slot).start()

    c3_copy(b, slot).wait()
    caches = (c1_ref, c2_ref, c3_buf.at[slot])

    def cache_rows(g, kv, h):
        x = caches[g][kv, h]
        return x.reshape(-1, hd).astype(BF16)

    def mask_for(g):
        n = caches[g].shape[2] * (caches[g].shape[3] if g == 2 else 1)
        row = lax.broadcasted_iota(jnp.int32, (SUBLANES, n), 0)
        col = lax.broadcasted_iota(jnp.int32, (SUBLANES, n), 1)
        if g == 0:
            return col >= row
        step = ATTN_GROUPS[1][1] if g == 1 else caches[g].shape[3]
        assert step & (step - 1) == 0
        return jnp.bitwise_and(col, step - 1) == row

    masks = [mask_for(g) for g in range(N_GROUPS)]
    qrow = lax.broadcasted_iota(jnp.int32, (SUBLANES, 1), 0)

    def scores(h):
        parts = []
        news = []
        for g in range(N_GROUPS):
            q = jnp.concatenate([q_ref[g, h], jnp.zeros((pad, hd), F32)], axis=0).astype(BF16)
            s = _dot_nt(q, cache_rows(g, 0, h)) * ATTN_SCALE
            parts.append((jnp.where(masks[g], s, NEG_INF), g))
            qf = q.astype(F32)
            kn = k_ref[g, h].astype(BF16).astype(F32)
            vn = v_ref[g, h].astype(BF16).astype(F32)
            if g == 0:
                for c in range(nq):
                    sn = jnp.sum(qf * kn[c:c + 1, :], axis=-1, keepdims=True) * ATTN_SCALE
                    news.append((jnp.where(qrow >= c, sn, NEG_INF), vn[c:c + 1, :]))
            else:
                knp = jnp.concatenate([kn, jnp.zeros((pad, hd), F32)], axis=0)
                sn = jnp.sum(qf * knp, axis=-1, keepdims=True) * ATTN_SCALE
                for c in range(nq):
                    news.append((jnp.where(qrow == c, sn, NEG_INF), vn[c:c + 1, :]))
        return parts, news

    def softmax(parts, news):
        m = parts[0][0].max(axis=-1, keepdims=True)
        for s, _ in parts[1:]:
            m = jnp.maximum(m, s.max(axis=-1, keepdims=True))
        for s, _ in news:
            m = jnp.maximum(m, s)
        es = [jnp.exp(s - m) for s, _ in parts]
        l = jnp.zeros((SUBLANES, 1), F32)
        for e in es:
            l = l + jnp.sum(e, axis=-1, keepdims=True)
        acc = jnp.zeros((SUBLANES, hd), F32)
        for s, v in news:
            e = jnp.exp(s - m)
            l = l + e
            acc = acc + e.astype(BF16).astype(F32) * v
        return es, l, acc

    def trip(it, carry):
        heads = [it * SAMPLE_HEADS_PER_TRIP + u for u in range(SAMPLE_HEADS_PER_TRIP)]
        staged = [scores(h) for h in heads]
        soft = [softmax(parts, news) for parts, news in staged]
        outs = []
        for h, (parts, _), (es, l, acc) in zip(heads, staged, soft):
            for e, (_, g) in zip(es, parts):
                acc = acc + _dot(e.astype(BF16), cache_rows(g, 1, h))
            outs.append((acc / l)[:nq, :])
        for h, out in zip(heads, outs):
            attn_ref[h] = out
        return carry

    lax.fori_loop(0, HEADS_PER_GROUP // SAMPLE_HEADS_PER_TRIP, trip, 0)


def _attn_sample_specs(q_s, c128, c512, c2048, seq_of):
    nb, _, nh, nq, hd = q_s.shape
    new_spec = pl.BlockSpec((None, N_GROUPS, nh, nq, hd), lambda *g: (seq_of(*g), 0, 0, 0, 0))
    for cache, (window, _) in zip((c128, c512, c2048), ATTN_GROUPS):
        assert cache.shape[4] == window, "the caches must hold a full window"
    assert nq <= ATTN_GROUPS[1][1]
    dil3 = ATTN_GROUPS[2][1]
    rows3 = c2048.shape[4] // dil3
    c2048 = c2048.reshape(1, nb, 2, nh, rows3, dil3, hd)
    in_specs = [
        new_spec, new_spec, new_spec,
        pl.BlockSpec((None, None, 2, nh, c128.shape[4], hd), lambda *g: (0, seq_of(*g), 0, 0, 0, 0)),
        pl.BlockSpec((None, None, 2, nh, c512.shape[4], hd), lambda *g: (0, seq_of(*g), 0, 0, 0, 0)),
        pl.BlockSpec(memory_space=pl.ANY),
    ]
    out_spec = pl.BlockSpec((None, nh, nq, hd), lambda *g: (seq_of(*g), 0, 0, 0))
    out_shape = jax.ShapeDtypeStruct((nb, nh, nq, hd), F32)
    scratch = [pltpu.VMEM((2, 2, nh, rows3, SUBLANES, hd), F32), pltpu.SemaphoreType.DMA((2,))]
    return c2048, in_specs, out_spec, out_shape, scratch


def _rglru_body(*refs, chained):
    if chained:
        (x_ref, y_ref, cw_ref, cb_ref, wri_ref, br_ref, bi_ref, lam_ref,
         rec_ref, hout_ref, sa, sb, sh, tail_ref, carry_ref) = refs
    else:
        (x_ref, y_ref, h0_ref, cw_ref, cb_ref, wri_ref, br_ref, bi_ref, lam_ref,
         rec_ref, hout_ref, sa, sb, sh) = refs
    tc, ch = x_ref.shape
    tiles = (tc // SUBLANES, SUBLANES, ch)
    step = pl.program_id(2) if chained else None
    row8 = lax.broadcasted_iota(jnp.int32, tiles, 1)

    cur = x_ref[...]
    if chained:
        @pl.when(step == 0)
        def _():
            tail_ref[...] = jnp.zeros_like(tail_ref)
            carry_ref[...] = jnp.zeros_like(carry_ref)

        prev = jnp.concatenate([tail_ref[...], cur[:tc - SUBLANES, :]], axis=0)
        tail_ref[...] = cur[tc - SUBLANES:, :]
    else:
        prev = cur
    cur = cur.reshape(tiles)
    prev = prev.reshape(tiles)

    xc = cb_ref[...]
    for j in range(CONV_W):
        d = CONV_W - 1 - j
        xs = cur if d == 0 else pltpu.roll(jnp.where(row8 >= SUBLANES - d, prev, cur), d, axis=1)
        xc = xc + xs * cw_ref[j:j + 1, :]

    xc2 = xc.reshape(tc, ch)
    pre_r, pre_i = [], []
    for n in range(ch // LRU_BLOCK_W):
        blk = xc2[:, n * LRU_BLOCK_W:(n + 1) * LRU_BLOCK_W].astype(BF16)
        ri = _dot(blk, wri_ref[n])
        pre_r.append(ri[:, :LRU_BLOCK_W])
        pre_i.append(ri[:, LRU_BLOCK_W:])
    r = jax.nn.sigmoid(jnp.concatenate(pre_r, axis=-1) + br_ref[...])
    i = jax.nn.sigmoid(jnp.concatenate(pre_i, axis=-1) + bi_ref[...])
    log_a = r * (-LRU_C * jax.nn.softplus(-lam_ref[...]))
    a = jnp.exp(log_a)
    th = jnp.tanh(log_a)
    p = -2.0 * th
    w = 1.0 - th
    mult = jnp.where(p > 0.0, p * lax.rsqrt(p * w), 0.0)
    b = mult * (i * xc2)

    if not chained:
        a = jnp.where(row8 < CONV_W, 1.0, a.reshape(tiles)).reshape(tc, ch)
        b = jnp.where(row8 < CONV_W, h0_ref[...].reshape(tiles), b.reshape(tiles)).reshape(tc, ch)

    slabs = ch // LANES
    for c in range(slabs):
        sa[c] = a[:, c * LANES:(c + 1) * LANES]
        sb[c] = b[:, c * LANES:(c + 1) * LANES]
    span = SCAN_TILES_PER_PASS
    tile_row = lax.broadcasted_iota(jnp.int32, (span, LANES), 0)
    for c in range(slabs):
        carry = carry_ref[c, 0:1, :] if chained else None
        for t0 in range(0, tc, span * SUBLANES):
            rows = [pl.ds(t0 + j, span, stride=SUBLANES) for j in range(SUBLANES)]
            a_rows = [sa[c, idx, :] for idx in rows]
            b_rows = [sb[c, idx, :] for idx in rows]
            acc_a, acc_b = [a_rows[0]], [b_rows[0]]
            for j in range(1, SUBLANES):
                acc_b.append(a_rows[j] * acc_b[-1] + b_rows[j])
                acc_a.append(a_rows[j] * acc_a[-1])
            if chained:
                ta, tb = acc_a[-1], acc_b[-1]
                shift = 1
                while shift < span:
                    valid = tile_row >= shift
                    tb = jnp.where(valid, ta * pltpu.roll(tb, shift, axis=0) + tb, tb)
                    ta = jnp.where(valid, ta * pltpu.roll(ta, shift, axis=0), ta)
                    shift *= 2
                state_out = ta * carry + tb
                state_in = jnp.where(tile_row == 0, carry, pltpu.roll(state_out, 1, axis=0))
                acc_b = [pa * state_in + pb for pa, pb in zip(acc_a, acc_b)]
                carry = state_out[span - 1:span, :]
            for j, idx in enumerate(rows):
                sh[c, idx, :] = acc_b[j]
        if chained:
            carry_ref[c] = jnp.broadcast_to(carry, (SUBLANES, LANES))
    h = jnp.concatenate([sh[c] for c in range(slabs)], axis=-1)

    rec_ref[...] = (h * jax.nn.gelu(y_ref[...])).astype(rec_ref.dtype)
    if chained:
        @pl.when(step == pl.num_programs(2) - 1)
        def _():
            hout_ref[...] = h[tc - 1:tc, :]
    else:
        hout_ref[...] = h


def _rglru_params_specs(ch, idx):
    nblk = ch // LRU_BLOCK_W
    return [
        pl.BlockSpec((CONV_W, ch), lambda *g: (0, idx(*g))),
        pl.BlockSpec((1, ch), lambda *g: (0, idx(*g))),
        pl.BlockSpec((nblk, LRU_BLOCK_W, 2 * LRU_BLOCK_W), lambda *g: (idx(*g), 0, 0)),
        pl.BlockSpec((1, ch), lambda *g: (0, idx(*g))),
        pl.BlockSpec((1, ch), lambda *g: (0, idx(*g))),
        pl.BlockSpec((1, ch), lambda *g: (0, idx(*g))),
    ]


N_RGLRU_IN = 8
N_RGLRU_SCRATCH = 5


def _rglru_prompt_attn_sample_body(*refs):
    rg_in, rest = refs[:N_RGLRU_IN], refs[N_RGLRU_IN:]
    at_in, rest = rest[:6], rest[6:]
    (rec_ref, hout_ref, attn_ref), rest = rest[:3], rest[3:]
    rg_scratch, at_scratch = rest[:N_RGLRU_SCRATCH], rest[N_RGLRU_SCRATCH:]
    _rglru_body(*rg_in, rec_ref, hout_ref, *rg_scratch, chained=True)
    n1, n2 = pl.num_programs(1), pl.num_programs(2)
    seq_id = (pl.program_id(0) * n1 + pl.program_id(1)) * n2 + pl.program_id(2)
    _attn_sample_body(*at_in, attn_ref, *at_scratch,
                      seq_id=seq_id, num_seqs=pl.num_programs(0) * n1 * n2)


def _rglru_prompt_attn_sample(z3, x_col, y_col, params, q_s, k_s, v_s, c128, c512, c2048,
                              *, d_rnn, tc, ch):
    batch, seq, _ = z3.shape
    nc = d_rnn // ch
    nt = seq // tc
    assert q_s.shape[0] == batch * nc * nt, "one sample sequence per recurrence step"
    c2048, at_specs, at_out_spec, at_out_shape, at_scratch = _attn_sample_specs(
        q_s, c128, c512, c2048, lambda b, c, t: (b * nc + c) * nt + t)
    rec, hlast, attn = pl.pallas_call(
        _rglru_prompt_attn_sample_body,
        grid=(batch, nc, nt),
        in_specs=[
            pl.BlockSpec((None, tc, ch), lambda b, c, t: (b, t, x_col // ch + c)),
            pl.BlockSpec((None, tc, ch), lambda b, c, t: (b, t, y_col // ch + c)),
        ] + _rglru_params_specs(ch, lambda b, c, t: c) + at_specs,
        out_specs=[
            pl.BlockSpec((None, tc, ch), lambda b, c, t: (b, t, c)),
            pl.BlockSpec((None, 1, ch), lambda b, c, t: (b, 0, c)),
            at_out_spec,
        ],
        out_shape=[jax.ShapeDtypeStruct((batch, seq, d_rnn), BF16),
                   jax.ShapeDtypeStruct((batch, 1, d_rnn), F32),
                   at_out_shape],
        scratch_shapes=[pltpu.VMEM((ch // LANES, tc, LANES), F32) for _ in range(3)]
        + [pltpu.VMEM((SUBLANES, ch), F32), pltpu.VMEM((ch // LANES, SUBLANES, LANES), F32)]
        + at_scratch,
        compiler_params=_params("arbitrary", "arbitrary", "arbitrary"),
        name="rglru_prompt_attn_sample",
    )(z3, z3, *params, q_s, k_s, v_s, c128, c512, c2048)
    return rec, hlast, attn


def _rglru_sample(x8, y8, h8, params, *, tc, ch):
    rows, d_rnn = x8.shape
    nc = d_rnn // ch
    row_spec = pl.BlockSpec((tc, ch), lambda c, t: (t, c))
    return pl.pallas_call(
        functools.partial(_rglru_body, chained=False),
        grid=(nc, rows // tc),
        in_specs=[row_spec, row_spec, row_spec] + _rglru_params_specs(ch, lambda c, t: c),
        out_specs=[row_spec, row_spec],
        out_shape=[jax.ShapeDtypeStruct((rows, d_rnn), F32),
                   jax.ShapeDtypeStruct((rows, d_rnn), F32)],
        scratch_shapes=[pltpu.VMEM((ch // LANES, tc, LANES), F32) for _ in range(3)],
        compiler_params=_params("parallel", "parallel"),
        name="rglru_sample",
    )(x8, y8, h8, *params)


def _merge_body(x_ref, attn_ref, rec_ref, ga0_ref, ga1_ref, gr0_ref, gr1_ref,
                wba_ref, wbr_ref, wout_ref, o_ref):
    pa = _dot(attn_ref[...].astype(BF16), wba_ref[...])
    pr = _dot(rec_ref[...].astype(BF16), wbr_ref[...])
    ga = jnp.concatenate([ga0_ref[...], ga1_ref[...]], axis=-1)
    gr = jnp.concatenate([gr0_ref[...], gr1_ref[...]], axis=-1)
    merged = jax.nn.sigmoid(ga) * pa + jax.nn.sigmoid(gr) * pr
    o_ref[...] = x_ref[...] + _dot(merged.astype(BF16), wout_ref[...])


def _merge(x, attn, rec, z, g_col, wba, wbr, wout, *, tm):
    m, d = x.shape
    tm = min(tm, m)
    half = d // 2
    gb = g_col // half

    def whole(w):
        return pl.BlockSpec(w.shape, lambda i: (0, 0), pipeline_mode=pl.Buffered(1))

    def gate_spec(k):
        return pl.BlockSpec((tm, half), lambda i, k=k: (i, gb + k))

    return pl.pallas_call(
        _merge_body,
        grid=(m // tm,),
        in_specs=[
            pl.BlockSpec((tm, d), lambda i: (i, 0)),
            pl.BlockSpec((tm, attn.shape[1]), lambda i: (i, 0)),
            pl.BlockSpec((tm, rec.shape[1]), lambda i: (i, 0)),
            gate_spec(0), gate_spec(1), gate_spec(2), gate_spec(3),
            whole(wba), whole(wbr), whole(wout),
        ],
        out_specs=pl.BlockSpec((tm, d), lambda i: (i, 0)),
        out_shape=jax.ShapeDtypeStruct((m, d), F32),
        compiler_params=_params("parallel"),
        name="merge",
    )(x, attn, rec, z, z, z, z, wba, wbr, wout)


def _ple_body(x_ref, pe_ref, lnp_ref, wpg_ref, wpp_ref, lnf_ref, y_ref):
    for r0 in range(0, x_ref.shape[0], ROW_BAND):
        rows = slice(r0, r0 + ROW_BAND)
        x = x_ref[rows, :]
        hn = _rmsnorm(x, lnp_ref[...]).astype(BF16)
        gate = jax.nn.sigmoid(_dot(hn, wpg_ref[...]))
        proj = _dot(pe_ref[rows, :].astype(BF16), wpp_ref[...])
        y_ref[rows, :] = _rmsnorm(x + gate * proj, lnf_ref[...])


def _ple(x, pe, lnp, wpg, wpp, lnf, *, tm):
    m, d = x.shape
    tm = min(tm, m)

    def whole(w):
        return pl.BlockSpec(w.shape, lambda i: (0, 0), pipeline_mode=pl.Buffered(1))

    return pl.pallas_call(
        _ple_body,
        grid=(m // tm,),
        in_specs=[
            pl.BlockSpec((tm, d), lambda i: (i, 0)),
            pl.BlockSpec((tm, pe.shape[1]), lambda i: (i, 0)),
            pl.BlockSpec((1, d), lambda i: (0, 0)),
            whole(wpg), whole(wpp),
            pl.BlockSpec((1, d), lambda i: (0, 0)),
        ],
        out_specs=pl.BlockSpec((tm, d), lambda i: (i, 0)),
        out_shape=jax.ShapeDtypeStruct((m, d), F32),
        compiler_params=_params("parallel"),
        name="ple",
    )(x, pe, lnp, wpg, wpp, lnf)


def kernel(x_prompt, x_sample, cache_kv_w128, cache_kv_w512, cache_kv_w2048, state_conv, state_rglru, p_prompt, p_sample, ln_ffn1, w_ffn1_gate, w_ffn1_up, w_ffn1_down, ln_mix, w_in, conv_w, conv_b, w_rgate, b_rgate, w_igate, b_igate, lru_lambda, w_branch_attn, w_branch_rec, w_out, ln_ffn2, w_ffn2_gate, w_ffn2_up, w_ffn2_down, ln_ple, w_ple_gate, w_ple_proj, ln_final):
    batch, seq, d = x_prompt.shape
    nb, nq, _ = x_sample.shape
    past_len = cache_kv_w2048.shape[4]
    d_rnn = conv_w.shape[-1]
    depth = w_in.shape[0]
    assert depth == 1

    xp = x_prompt.reshape(batch * seq, d)
    xs = x_sample.reshape(nb * nq, d)
    cos_p, sin_p = _rope_tables(jnp.arange(seq))
    cos_s, sin_s = _rope_tables(past_len + (jnp.arange(nb * nq) % nq))

    l = 0
    row = lambda v: v[l].reshape(1, -1)
    wri = jnp.concatenate([w_rgate[l], w_igate[l]], axis=-1).astype(BF16)
    rg_params = (conv_w[l], row(conv_b), wri, row(b_rgate), row(b_igate), row(lru_lambda))
    w1 = (w_ffn1_gate, w_ffn1_up, w_ffn1_down)
    w2 = (w_ffn2_gate, w_ffn2_up, w_ffn2_down)
    ffn_tiles = dict(tm=1024, tf=256)
    ffn_tiles_sample = dict(tm=512, tf=512)
    win = w_in[l].astype(BF16)
    wba = w_branch_attn[l].astype(BF16)
    wbr = w_branch_rec[l].astype(BF16)
    wout = w_out[l].astype(BF16)
    wpg = w_ple_gate[l].astype(BF16)
    wpp = w_ple_proj[l].astype(BF16)
    lnf = ln_final.reshape(1, -1)

    x_col = 3 * QKV_WIDTH
    y_col = x_col + d_rnn
    g_col = y_col + d_rnn
    tm = 512

    xs = _ffn(xs, row(ln_ffn1), *w1, **ffn_tiles_sample)
    zs = _inproj(xs, row(ln_mix), win, cos_s, sin_s, tm=tm, tn=512)
    qkv = zs[:, :3 * QKV_WIDTH].reshape(nb, nq, 3, N_GROUPS, HEADS_PER_GROUP, HEAD_DIM)
    qkv = qkv.transpose(2, 0, 3, 4, 1, 5)

    xp = _ffn(xp, row(ln_ffn1), *w1, **ffn_tiles)
    zp = _inproj(xp, row(ln_mix), win, cos_p, sin_p, tm=1024, tn=1024)
    zp3 = zp.reshape(batch, seq, -1)
    attn_p, kv1_p, kv2_p, kv3_p = _attn_prompt(zp3, batch, seq)
    rec_p, hlast_p, attn_s = _rglru_prompt_attn_sample(
        zp3, x_col, y_col, rg_params, qkv[0], qkv[1], qkv[2],
        cache_kv_w128, cache_kv_w512, cache_kv_w2048, d_rnn=d_rnn, tc=256, ch=1024)
    conv_p = zp3[:, seq - (CONV_W - 1):, x_col:x_col + d_rnn][None]
    xp = _merge(xp, attn_p.reshape(batch * seq, -1), rec_p.reshape(batch * seq, -1), zp, g_col,
                wba, wbr, wout, tm=256)
    xp = _ffn(xp, row(ln_ffn2), *w2, **ffn_tiles)
    yp = _ple(xp, p_prompt[l].reshape(batch * seq, -1), row(ln_ple), wpg, wpp, lnf, tm=512)

    attn_s = attn_s.transpose(0, 2, 1, 3).reshape(nb * nq, -1)
    kv_s = [jnp.stack([qkv[1][:, g], qkv[2][:, g]], axis=1)[None] for g in range(N_GROUPS)]

    x_rec = zs[:, x_col:x_col + d_rnn].reshape(nb, nq, d_rnn)
    y_rec = zs[:, y_col:y_col + d_rnn].reshape(nb, nq, d_rnn)
    zero1 = jnp.zeros((nb, 1, d_rnn), F32)
    x8 = jnp.concatenate([zero1, state_conv[l], x_rec], axis=1).reshape(nb * SUBLANES, d_rnn)
    y8 = jnp.concatenate([jnp.zeros((nb, CONV_W, d_rnn), F32), y_rec], axis=1).reshape(nb * SUBLANES, d_rnn)
    h8 = jnp.concatenate([jnp.zeros((nb, CONV_W - 1, d_rnn), F32), state_rglru[l][:, None],
                          jnp.zeros((nb, nq, d_rnn), F32)], axis=1).reshape(nb * SUBLANES, d_rnn)
    rec8, hs8 = _rglru_sample(x8, y8, h8, rg_params, tc=256, ch=1024)
    rec_s = rec8.reshape(nb, SUBLANES, d_rnn)[:, CONV_W:].reshape(nb * nq, d_rnn)
    hlast_s = hs8.reshape(nb, SUBLANES, d_rnn)[:, SUBLANES - 1][None]
    conv_s = jnp.concatenate([state_conv[l], x_rec], axis=1)[:, -(CONV_W - 1):][None]
    xs = _merge(xs, attn_s, rec_s, zs, g_col, wba, wbr, wout, tm=256)
    xs = _ffn(xs, row(ln_ffn2), *w2, **ffn_tiles_sample)
    ys = _ple(xs, p_sample[l].reshape(nb * nq, -1), row(ln_ple), wpg, wpp, lnf, tm=512)

    return (yp.reshape(batch, seq, d), ys.reshape(nb, nq, d),
            kv1_p, kv2_p, kv3_p, conv_p, hlast_p.reshape(1, batch, d_rnn),
            kv_s[0], kv_s[1], kv_s[2], conv_s, hlast_s)
```
